```python
import math
import jax, jax.numpy as jnp
from jax import lax
import numpy as np

D_MODEL = 1024
BATCH = 4
SEQ = 4096
DEPTH = 2

GRID_W = 64
CTX_LEN = 256
N_MIXERS = 2
EPS = 1e-6
MLA_HEADS = 16
Q_LORA = 512
KV_LORA = 256
QK_NOPE = 64
QK_ROPE = 32
V_HEAD = 64
ROPE_THETA = 10000.0
Q_BLOCK = 128
SSD_EXPAND = 2
SSD_INNER = SSD_EXPAND * D_MODEL
SSD_HEADDIM = 64
SSD_HEADS = SSD_INNER // SSD_HEADDIM
SSD_GROUPS = 4
SSD_STATE = 128
SSD_CONV = 5
SSD_CONV_DIM = SSD_INNER + 2 * SSD_GROUPS * SSD_STATE
SSD_IN_DIM = SSD_INNER + SSD_CONV_DIM + 2 * SSD_HEADS
CHUNK = 128
FFN_HIDDEN = ((8 * D_MODEL // 3 + 255) // 256) * 256

kernel_name = "hybrid_mla_ssd_diffusion_block"


def rms_norm(x, g):
    xf = x.astype(jnp.float32)
    y = xf * lax.rsqrt(jnp.mean(xf * xf, axis=-1, keepdims=True) + EPS)
    return (y * g.astype(jnp.float32)).astype(x.dtype)


def rope_1d(x, pos):
    d = x.shape[-1]
    half = d // 2
    inv = ROPE_THETA ** (-jnp.arange(half, dtype=jnp.float32) * 2.0 / d)
    ang = pos.astype(jnp.float32)[:, None] * inv[None, :]
    cos = jnp.cos(ang)[None, :, None, :]
    sin = jnp.sin(ang)[None, :, None, :]
    xf = x.astype(jnp.float32)
    x1, x2 = xf[..., :half], xf[..., half:]
    return jnp.concatenate([x1 * cos - x2 * sin, x2 * cos + x1 * sin], axis=-1).astype(x.dtype)


def rope_2d(x, row, col):
    h = x.shape[-1] // 2
    return jnp.concatenate([rope_1d(x[..., :h], row), rope_1d(x[..., h:], col)], axis=-1)


def sdpa(q, k, v):
    s = jnp.einsum('bqhd,bshd->bhqs', q, k).astype(jnp.float32) * (1.0 / math.sqrt(q.shape[-1]))
    p = jax.nn.softmax(s, axis=-1).astype(v.dtype)
    return jnp.einsum('bhqs,bshd->bqhd', p, v)


def blocked_sdpa(q, k, v):
    bsz, L, H, dk = q.shape
    nb = L // Q_BLOCK
    qb = q.reshape(bsz, nb, Q_BLOCK, H, dk).transpose(1, 0, 2, 3, 4)
    out = lax.map(lambda blk: sdpa(blk, k, v), qb)
    return out.transpose(1, 0, 2, 3, 4).reshape(bsz, L, H * v.shape[-1])


def mla_mixer(hx, hc, row, col, w_in, q_norm, w_uq, kv_norm, w_ukv, w_o, need_ctx):
    bsz, Lc = hc.shape[0], hc.shape[1]
    h = jnp.concatenate([hc, hx], axis=1)
    T = h.shape[1]
    proj = h @ w_in
    cq, ckv, kr = jnp.split(proj, [Q_LORA, Q_LORA + KV_LORA], axis=-1)
    q = (rms_norm(cq, q_norm) @ w_uq).reshape(bsz, T, MLA_HEADS, QK_NOPE + QK_ROPE)
    kv = (rms_norm(ckv, kv_norm) @ w_ukv).reshape(bsz, T, MLA_HEADS, QK_NOPE + V_HEAD)
    k_nope, v = kv[..., :QK_NOPE], kv[..., QK_NOPE:]
    q_nope, q_rope = q[..., :QK_NOPE], q[..., QK_NOPE:]
    q_rope = jnp.concatenate([q_rope[:, :Lc], rope_2d(q_rope[:, Lc:], row, col)], axis=1)
    kr = kr[:, :, None, :]
    kr = jnp.concatenate([kr[:, :Lc], rope_2d(kr[:, Lc:], row, col)], axis=1)
    k = jnp.concatenate([k_nope, jnp.broadcast_to(kr, (bsz, T, MLA_HEADS, QK_ROPE))], axis=-1)
    q = jnp.concatenate([q_nope, q_rope], axis=-1)
    out_x = blocked_sdpa(q[:, Lc:], k, v) @ w_o
    out_c = None
    if need_ctx:
        out_c = sdpa(q[:, :Lc], k[:, :Lc], v[:, :Lc]).reshape(bsz, Lc, MLA_HEADS * V_HEAD) @ w_o
    return out_x, out_c


def dw_conv(u, w, b):
    C = u.shape[-1]
    K = w.shape[0]
    y = lax.conv_general_dilated(u, w[:, None, :].astype(u.dtype), window_strides=(1,),
                                 padding=[(K // 2, K // 2)], dimension_numbers=('NWC', 'WIO', 'NWC'),
                                 feature_group_count=C)
    return y + b


def ssd_chunked(xs, dt, A, bm, cm, h0):
    bsz, L, H, P = xs.shape
    G, N = bm.shape[2], bm.shape[3]
    Hg = H // G
    nc = L // CHUNK
    xc = xs.reshape(bsz, nc, CHUNK, G, Hg, P)
    dtc = dt.reshape(bsz, nc, CHUNK, G, Hg)
    bc = bm.reshape(bsz, nc, CHUNK, G, N)
    cc = cm.reshape(bsz, nc, CHUNK, G, N)
    cum = jnp.cumsum(dtc * A.reshape(G, Hg), axis=2)
    lower = jnp.tril(jnp.ones((CHUNK, CHUNK), dtype=bool))[None, None, :, :, None, None]
    seg = cum[:, :, :, None] - cum[:, :, None, :]
    decay = jnp.exp(jnp.where(lower, seg, -jnp.inf))
    cb = jnp.einsum('bcqgn,bckgn->bcqkg', cc, bc).astype(jnp.float32)
    w = cb[..., None] * decay * dtc[:, :, None]
    y_diag = jnp.einsum('bcqkgh,bckghp->bcqghp', w, xc)
    to_end = jnp.exp(cum[:, :, -1:] - cum) * dtc
    states = jnp.einsum('bcqgn,bcqgh,bcqghp->bcghpn', bc, to_end, xc)
    chunk_decay = jnp.exp(cum[:, :, -1])

    def step(h, inp):
        s, d = inp
        return d[..., None, None] * h + s, h

    h_last, h_start = lax.scan(step, h0.reshape(bsz, G, Hg, P, N),
                               (jnp.moveaxis(states, 1, 0), jnp.moveaxis(chunk_decay, 1, 0)))
    h_start = jnp.moveaxis(h_start, 0, 1)
    y_off = jnp.einsum('bcqgn,bcghpn,bcqgh->bcqghp', cc, h_start, jnp.exp(cum))
    y = (y_diag + y_off).reshape(bsz, L, H, P).astype(xs.dtype)
    return y, h_last.reshape(bsz, H, P, N)


def ssd_mixer(hx, hc, w_in, conv_w, conv_b, dt_bias, a_log, d_skip, norm_g, w_o, need_ctx):
    bsz, Lc = hc.shape[0], hc.shape[1]
    proj = jnp.concatenate([hc, hx], axis=1) @ w_in
    T = proj.shape[1]
    z, xbc, dt_raw = jnp.split(proj, [SSD_INNER, SSD_INNER + SSD_CONV_DIM], axis=-1)
    xbc = jax.nn.silu(jnp.concatenate([dw_conv(xbc[:, :Lc], conv_w, conv_b),
                                       dw_conv(xbc[:, Lc:], conv_w, conv_b)], axis=1))
    xs, bm, cm = jnp.split(xbc, [SSD_INNER, SSD_INNER + SSD_GROUPS * SSD_STATE], axis=-1)
    xs = xs.reshape(bsz, T, SSD_HEADS, SSD_HEADDIM)
    bm = bm.reshape(bsz, T, SSD_GROUPS, SSD_STATE)
    cm = cm.reshape(bsz, T, SSD_GROUPS, SSD_STATE)
    dt = jax.nn.softplus(dt_raw.reshape(bsz, T, 2, SSD_HEADS).astype(jnp.float32)
                         + dt_bias.astype(jnp.float32))
    A = -jnp.exp(a_log.astype(jnp.float32))
    h0 = jnp.zeros((bsz, SSD_HEADS, SSD_HEADDIM, SSD_STATE), jnp.float32)
    fl = lambda a: jnp.flip(a, axis=1)
    xs_c, xs_x = xs[:, :Lc], xs[:, Lc:]
    bm_c, bm_x = bm[:, :Lc], bm[:, Lc:]
    cm_c, cm_x = cm[:, :Lc], cm[:, Lc:]
    dt_c, dt_x = dt[:, :Lc], dt[:, Lc:]
    yc_f, hcf = ssd_chunked(xs_c, dt_c[:, :, 0], A[0], bm_c, cm_c, h0)
    yx_f, _ = ssd_chunked(xs_x, dt_x[:, :, 0], A[0], bm_x, cm_x, hcf)
    yc_b, hcb = ssd_chunked(fl(xs_c), fl(dt_c[:, :, 1]), A[1], fl(bm_c), fl(cm_c), h0)
    yx_b, _ = ssd_chunked(fl(xs_x), fl(dt_x[:, :, 1]), A[1], fl(bm_x), fl(cm_x), hcb)
    dsk = d_skip[:, None]

    def gate_out(y, zz):
        y = y.reshape(bsz, y.shape[1], SSD_INNER)
        return rms_norm(y * jax.nn.silu(zz), norm_g) @ w_o

    out_x = gate_out(yx_f + fl(yx_b) + dsk * xs_x, z[:, Lc:])
    out_c = None
    if need_ctx:
        out_c = gate_out(yc_f + fl(yc_b) + dsk * xs_c, z[:, :Lc])
    return out_x, out_c


def swiglu(h, w1, w3, w2):
    return (jax.nn.silu(h @ w1) * (h @ w3)) @ w2


def setup_inputs(seed: int = 0) -> dict:
    key = jax.random.key(seed)
    ks = jax.random.split(key, 32)
    NA = (DEPTH + 1) // 2
    NB = DEPTH // 2
    f32 = jnp.float32
    nrm = lambda k, shape, s: jax.random.normal(k, shape, f32) * s
    gain = lambda k, shape: 1.0 + 0.02 * jax.random.normal(k, shape, f32)
    dt0 = jnp.exp(jax.random.uniform(ks[22], (NB, 2, SSD_HEADS), f32, math.log(1e-3), math.log(1e-1)))
    return {
        "x": nrm(ks[0], (BATCH, SEQ, D_MODEL), 1.0),
        "c": nrm(ks[1], (BATCH, D_MODEL), 1.0),
        "ctx": nrm(ks[2], (BATCH, CTX_LEN, D_MODEL), 1.0),
        "c_ctx": nrm(ks[3], (D_MODEL,), 1.0),
        "ada_w": nrm(ks[4], (DEPTH, D_MODEL, 6 * D_MODEL), D_MODEL ** -0.5),
        "ada_b": nrm(ks[5], (DEPTH, 6 * D_MODEL), 0.02),
        "norm_mix": gain(ks[6], (DEPTH, D_MODEL)),
        "norm_ffn": gain(ks[7], (DEPTH, D_MODEL)),
        "ffn_w1": nrm(ks[8], (DEPTH, D_MODEL, FFN_HIDDEN), D_MODEL ** -0.5),
        "ffn_w3": nrm(ks[9], (DEPTH, D_MODEL, FFN_HIDDEN), D_MODEL ** -0.5),
        "ffn_w2": nrm(ks[10], (DEPTH, FFN_HIDDEN, D_MODEL), FFN_HIDDEN ** -0.5),
        "mla_w_in": nrm(ks[11], (NA, D_MODEL, Q_LORA + KV_LORA + QK_ROPE), D_MODEL ** -0.5),
        "mla_q_norm": gain(ks[12], (NA, Q_LORA)),
        "mla_w_uq": nrm(ks[13], (NA, Q_LORA, MLA_HEADS * (QK_NOPE + QK_ROPE)), Q_LORA ** -0.5),
        "mla_kv_norm": gain(ks[14], (NA, KV_LORA)),
        "mla_w_ukv": nrm(ks[15], (NA, KV_LORA, MLA_HEADS * (QK_NOPE + V_HEAD)), KV_LORA ** -0.5),
        "mla_w_o": nrm(ks[16], (NA, MLA_HEADS * V_HEAD, D_MODEL), (MLA_HEADS * V_HEAD) ** -0.5),
        "ssd_w_in": nrm(ks[17], (NB, D_MODEL, SSD_IN_DIM), D_MODEL ** -0.5),
        "ssd_conv_w": nrm(ks[18], (NB, SSD_CONV, SSD_CONV_DIM), SSD_CONV ** -0.5),
        "ssd_conv_b": nrm(ks[19], (NB, SSD_CONV_DIM), 0.02),
        "ssd_dt_bias": dt0 + jnp.log(-jnp.expm1(-dt0)),
        "ssd_a_log": jnp.log(jax.random.uniform(ks[20], (NB, 2, SSD_HEADS), f32, 1.0, 16.0)),
        "ssd_d": gain(ks[21], (NB, SSD_HEADS)),
        "ssd_norm": gain(ks[23], (NB, SSD_INNER)),
        "ssd_w_o": nrm(ks[24], (NB, SSD_INNER, D_MODEL), SSD_INNER ** -0.5),
        "final_norm": gain(ks[25], (D_MODEL,)),
    }


def reference(x, c, ctx, c_ctx, ada_w, ada_b, norm_mix, norm_ffn, ffn_w1, ffn_w3, ffn_w2,
              mla_w_in, mla_q_norm, mla_w_uq, mla_kv_norm, mla_w_ukv, mla_w_o,
              ssd_w_in, ssd_conv_w, ssd_conv_b, ssd_dt_bias, ssd_a_log, ssd_d, ssd_norm, ssd_w_o,
              final_norm):
    L = x.shape[1]
    rows = L // GRID_W
    row = jnp.repeat(jnp.arange(rows), GRID_W)
    col = jnp.tile(jnp.arange(GRID_W), rows)
    sc = jax.nn.silu(c)
    scc = jax.nn.silu(c_ctx)
    for i in range(DEPTH):
        need_ctx = i < DEPTH - 1
        mod_x = jnp.split((sc @ ada_w[i] + ada_b[i])[:, None, :], 6, axis=-1)
        mod_c = jnp.split((scc @ ada_w[i] + ada_b[i])[None, None, :], 6, axis=-1)
        hx = rms_norm(x, norm_mix[i]) * (1.0 + mod_x[1]) + mod_x[0]
        hc = rms_norm(ctx, norm_mix[i]) * (1.0 + mod_c[1]) + mod_c[0]
        j = i // N_MIXERS
        if i % N_MIXERS == 0:
            out_x, out_c = mla_mixer(hx, hc, row, col, mla_w_in[j], mla_q_norm[j], mla_w_uq[j],
                                     mla_kv_norm[j], mla_w_ukv[j], mla_w_o[j], need_ctx)
        else:
            out_x, out_c = ssd_mixer(hx, hc, ssd_w_in[j], ssd_conv_w[j], ssd_conv_b[j], ssd_dt_bias[j],
                                     ssd_a_log[j], ssd_d[j], ssd_norm[j], ssd_w_o[j], need_ctx)
        x = x + mod_x[2] * out_x
        hx = rms_norm(x, norm_ffn[i]) * (1.0 + mod_x[4]) + mod_x[3]
        x = x + mod_x[5] * swiglu(hx, ffn_w1[i], ffn_w3[i], ffn_w2[i])
        if need_ctx:
            ctx = ctx + mod_c[2] * out_c
            hc = rms_norm(ctx, norm_ffn[i]) * (1.0 + mod_c[4]) + mod_c[3]
            ctx = ctx + mod_c[5] * swiglu(hc, ffn_w1[i], ffn_w3[i], ffn_w2[i])
    return rms_norm(x, final_norm)
```

```python
import functools
import math

import numpy as np
import jax
import jax.numpy as jnp
from jax import lax
from jax.experimental import pallas as pl
from jax.experimental.pallas import tpu as pltpu

D_MODEL = 1024
DEPTH = 2
GRID_W = 64
EPS = 1e-6
MLA_HEADS = 16
Q_LORA = 512
KV_LORA = 256
QK_NOPE = 64
QK_ROPE = 32
V_HEAD = 64
ROPE_THETA = 10000.0
SSD_INNER = 2 * D_MODEL
SSD_HEADDIM = 64
SSD_HEADS = SSD_INNER // SSD_HEADDIM
SSD_GROUPS = 4
SSD_STATE = 128
SSD_CONV = 5
SSD_CONV_DIM = SSD_INNER + 2 * SSD_GROUPS * SSD_STATE
CHUNK = 128
FFN_HIDDEN = ((8 * D_MODEL // 3 + 255) // 256) * 256

LANES = 128
V7X_VMEM_BYTES = 64 * 1024 * 1024
VMEM_LIMIT_CAP = 56 * 1024 * 1024

F32 = jnp.float32
BF16 = jnp.bfloat16
HEAD_LANES = 2 * QK_NOPE
HEADS_PER_GROUP = SSD_HEADS // SSD_GROUPS
GROUP_W = HEADS_PER_GROUP * SSD_HEADDIM


def _cparams(semantics, vmem_bytes):
    return pltpu.CompilerParams(
        dimension_semantics=semantics,
        vmem_limit_bytes=int(min(VMEM_LIMIT_CAP, vmem_bytes)),
    )


def _const_spec(shape):
    nd = len(shape)
    return pl.BlockSpec(shape, lambda *_: (0,) * nd, pipeline_mode=pl.Buffered(1))


def _rms(x, g):
    return x * lax.rsqrt(jnp.mean(x * x, axis=-1, keepdims=True) + EPS) * g


def _silu(x):
    return x * (1.0 / (1.0 + jnp.exp(-x)))


def _dot(a, b):
    return jnp.dot(a, b, preferred_element_type=F32)


def _dot_nt(a, b):
    return lax.dot_general(a, b, (((1,), (1,)), ((), ())), preferred_element_type=F32)


def _dot_f32(a, b):
    return jnp.dot(a, b, preferred_element_type=F32, precision=lax.Precision.HIGHEST)


def _ada_kernel(c_ref, w_ref, b_ref, o_ref):
    s = _silu(c_ref[...])
    o_ref[...] = _dot_f32(s, w_ref[...]) + b_ref[...]


def _ada(cc, ada_w, ada_b):
    n_tiles = ada_w.shape[2] // D_MODEL
    return pl.pallas_call(
        _ada_kernel,
        grid=(DEPTH, n_tiles),
        in_specs=[
            pl.BlockSpec((8, D_MODEL), lambda l, j: (0, 0)),
            pl.BlockSpec((None, D_MODEL, D_MODEL), lambda l, j: (l, 0, j)),
            pl.BlockSpec((None, 1, D_MODEL), lambda l, j: (l, 0, j)),
        ],
        out_specs=pl.BlockSpec((None, 8, D_MODEL), lambda l, j: (l, 0, j)),
        out_shape=jax.ShapeDtypeStruct((DEPTH, 8, ada_w.shape[2]), F32),
        compiler_params=_cparams(("arbitrary", "arbitrary"), 24 << 20),
        name="ada",
    )(cc, ada_w, ada_b.reshape(DEPTH, 1, -1))


def _mod_spec(mod):
    if mod.shape[0] == 1:
        return pl.BlockSpec((None, 6, D_MODEL), lambda b, i: (0, 0, 0))
    return pl.BlockSpec((None, 6, D_MODEL), lambda b, i: (b, 0, 0))


def _mla_proj_kernel(x_ref, mod_ref, g_ref, win_ref, qg_ref, wq_ref, kvg_ref, wk_ref, wv_ref,
                     tq_ref, tkc_ref, tks_ref, q_ref, k_ref, v_ref):
    x = x_ref[...]
    h = _rms(x, g_ref[...]) * (1.0 + mod_ref[1:2, :]) + mod_ref[0:1, :]
    p = _dot(h.astype(BF16), win_ref[...])
    cq = p[:, :Q_LORA]
    ckv = p[:, Q_LORA:Q_LORA + KV_LORA]
    kr_a = p[:, Q_LORA + KV_LORA:Q_LORA + KV_LORA + LANES]
    kr_b = p[:, Q_LORA + KV_LORA + LANES:]
    cqn = _rms(cq, qg_ref[...]).astype(BF16)
    ckvn = _rms(ckv, kvg_ref[...]).astype(BF16)
    qf = _dot(cqn, wq_ref[...])
    kf = _dot(ckvn, wk_ref[...])
    kr = kr_a * tkc_ref[...] + kr_b * tks_ref[...]
    tq = tq_ref[...]
    for hd in range(MLA_HEADS):
        sl = slice(hd * HEAD_LANES, (hd + 1) * HEAD_LANES)
        q_ref[:, sl] = (qf[:, sl] * tq).astype(BF16)
        k_ref[:, sl] = (kf[:, sl] + kr).astype(BF16)
    v_ref[...] = _dot(ckvn, wv_ref[...]).astype(BF16)


def _mla_proj(x, mod, gain, w, tabs, tm):
    bsz, seq, _ = x.shape
    hw = MLA_HEADS * HEAD_LANES
    row = lambda n: pl.BlockSpec((None, tm, n), lambda b, i: (b, i, 0))
    tab = pl.BlockSpec((tm, LANES), lambda b, i: (i, 0))
    return pl.pallas_call(
        _mla_proj_kernel,
        grid=(bsz, seq // tm),
        in_specs=[
            row(D_MODEL), _mod_spec(mod), _const_spec((1, D_MODEL)),
            _const_spec(w["win"].shape), _const_spec((1, Q_LORA)), _const_spec(w["wq"].shape),
            _const_spec((1, KV_LORA)), _const_spec(w["wk"].shape), _const_spec(w["wv"].shape),
            tab, tab, tab,
        ],
        out_specs=[row(hw), row(hw), row(MLA_HEADS * V_HEAD)],
        out_shape=[
            jax.ShapeDtypeStruct((bsz, seq, hw), BF16),
            jax.ShapeDtypeStruct((bsz, seq, hw), BF16),
            jax.ShapeDtypeStruct((bsz, seq, MLA_HEADS * V_HEAD), BF16),
        ],
        compiler_params=_cparams(("arbitrary", "arbitrary"), 40 << 20),
        name="mla_proj",
    )(x, mod, gain, w["win"], w["qg"], w["wq"], w["kvg"], w["wk"], w["wv"],
      tabs["q"], tabs["kc"], tabs["ks"])


def _attn_kernel(*refs, n_kv, tk):
    q_ref = refs[0]
    kv_refs = [(refs[1 + 2 * j], refs[2 + 2 * j]) for j in range(n_kv)]
    o_ref = refs[1 + 2 * n_kv]
    s_ref = refs[2 + 2 * n_kv]
    tq = q_ref.shape[0]
    lane = lax.broadcasted_iota(jnp.int32, (1, LANES), 1)
    chunks = []
    col = 0
    for k_ref, v_ref in kv_refs:
        rows = k_ref.shape[0]
        step = min(tk, rows)
        for r0 in range(0, rows, step):
            chunks.append((k_ref, v_ref, r0, step, col))
            col += step
    out = jnp.zeros((tq, LANES), F32)
    for hh in range(2):
        hs = slice(hh * HEAD_LANES, (hh + 1) * HEAD_LANES)
        qh = q_ref[:, hs]
        m = jnp.full((tq, 1), -jnp.inf, F32)
        for k_ref, _, r0, rows, c0 in chunks:
            s = _dot_nt(qh, k_ref[r0:r0 + rows, hs])
            s_ref[hh, :, c0:c0 + rows] = s
            m = jnp.maximum(m, jnp.max(s, axis=-1, keepdims=True))
        vmask = (lane < V_HEAD) if hh == 0 else (lane >= V_HEAD)
        l = jnp.zeros((tq, 1), F32)
        acc = jnp.zeros((tq, LANES), F32)
        for _, v_ref, r0, rows, c0 in chunks:
            p = jnp.exp(s_ref[hh, :, c0:c0 + rows] - m)
            l = l + jnp.sum(p, axis=-1, keepdims=True)
            vv = v_ref[r0:r0 + rows, :]
            vv = jnp.where(vmask, vv, jnp.zeros_like(vv))
            acc = acc + _dot(p.astype(BF16), vv)
        out = out + acc / l
    o_ref[...] = out.astype(BF16)


def _attn(q, kvs, tq, tk):
    bsz, lq, _ = q.shape
    n_pairs = MLA_HEADS // 2
    total = sum(k.shape[1] for k, _ in kvs)
    in_specs = [pl.BlockSpec((None, tq, 2 * HEAD_LANES), lambda b, h, i: (b, i, h))]
    args = [q]
    for k, v in kvs:
        t = k.shape[1]
        in_specs.append(pl.BlockSpec((None, t, 2 * HEAD_LANES), lambda b, h, i: (b, 0, h)))
        in_specs.append(pl.BlockSpec((None, t, 2 * V_HEAD), lambda b, h, i: (b, 0, h)))
        args += [k, v]
    vmem = 2 * tq * total * 4 + 4 * total * (2 * HEAD_LANES + 2 * V_HEAD) * 2 + (16 << 20)
    return pl.pallas_call(
        functools.partial(_attn_kernel, n_kv=len(kvs), tk=tk),
        grid=(bsz, n_pairs, lq // tq),
        in_specs=in_specs,
        out_specs=pl.BlockSpec((None, tq, 2 * V_HEAD), lambda b, h, i: (b, i, h)),
        out_shape=jax.ShapeDtypeStruct((bsz, lq, MLA_HEADS * V_HEAD), BF16),
        scratch_shapes=[pltpu.VMEM((2, tq, total), F32)],
        compiler_params=_cparams(("arbitrary", "arbitrary", "arbitrary"), vmem),
        name="attn",
    )(*args)


def _post_kernel(*refs, ssd, final):
    it = iter(refs)
    x_ref = next(it)
    if ssd:
        y_ref, z_ref, ng_ref = next(it), next(it), next(it)
    else:
        a_ref = next(it)
    mod_ref, wo_ref, gf_ref, w1_ref, w3_ref, w2_ref = (next(it) for _ in range(6))
    fg_ref = next(it) if final else None
    o_ref = next(it)

    if ssd:
        gated = y_ref[...] * _silu(z_ref[...].astype(F32))
        a = _rms(gated, ng_ref[...]).astype(BF16)
    else:
        a = a_ref[...]
    x1 = x_ref[...] + mod_ref[2:3, :] * _dot(a, wo_ref[...])
    hx = (_rms(x1, gf_ref[...]) * (1.0 + mod_ref[4:5, :]) + mod_ref[3:4, :]).astype(BF16)
    u = _dot(hx, w1_ref[...])
    w = _dot(hx, w3_ref[...])
    t = (_silu(u) * w).astype(BF16)
    x2 = x1 + mod_ref[5:6, :] * _dot(t, w2_ref[...])
    if final:
        x2 = _rms(x2, fg_ref[...])
    o_ref[...] = x2


def _post(x, mix_in, mod, w, tm, final_gain=None):
    bsz, seq, _ = x.shape
    ssd = isinstance(mix_in, tuple)
    row = lambda n: pl.BlockSpec((None, tm, n), lambda b, i: (b, i, 0))
    in_specs = [row(D_MODEL)]
    args = [x]
    if ssd:
        y, z, ng = mix_in
        in_specs += [row(SSD_INNER), row(SSD_INNER), _const_spec((1, SSD_INNER))]
        args += [y, z, ng]
    else:
        in_specs.append(row(mix_in.shape[2]))
        args.append(mix_in)
    in_specs += [_mod_spec(mod), _const_spec(w["wo"].shape), _const_spec((1, D_MODEL)),
                 _const_spec(w["w1"].shape), _const_spec(w["w3"].shape), _const_spec(w["w2"].shape)]
    args += [mod, w["wo"], w["gf"], w["w1"], w["w3"], w["w2"]]
    if final_gain is not None:
        in_specs.append(_const_spec((1, D_MODEL)))
        args.append(final_gain)
    return pl.pallas_call(
        functools.partial(_post_kernel, ssd=ssd, final=final_gain is not None),
        grid=(bsz, seq // tm),
        in_specs=in_specs,
        out_specs=row(D_MODEL),
        out_shape=jax.ShapeDtypeStruct((bsz, seq, D_MODEL), F32),
        compiler_params=_cparams(("arbitrary", "arbitrary"), VMEM_LIMIT_CAP),
        name="post_ssd" if ssd else "post_mla",
    )(*args)


def _ssd_in_kernel(x_ref, mod_ref, g_ref, wz_ref, wx_ref, wdh_ref, wdl_ref, db_ref,
                   z_ref, xbc_ref, dt_ref):
    h = _rms(x_ref[...], g_ref[...]) * (1.0 + mod_ref[1:2, :]) + mod_ref[0:1, :]
    hb = h.astype(BF16)
    z_ref[...] = _dot(hb, wz_ref[...]).astype(BF16)
    xbc_ref[...] = _dot(hb, wx_ref[...])
    hl = (h - hb.astype(F32)).astype(BF16)
    raw = _dot(hb, wdh_ref[...]) + _dot(hl, wdh_ref[...]) + _dot(hb, wdl_ref[...])
    raw = raw + db_ref[...]
    dt_ref[...] = jnp.maximum(raw, 0.0) + jnp.log(1.0 + jnp.exp(-jnp.abs(raw)))


def _ssd_in(x, mod, w, tm):
    bsz, seq, _ = x.shape
    row = lambda n: pl.BlockSpec((None, tm, n), lambda b, i: (b, i, 0))
    return pl.pallas_call(
        _ssd_in_kernel,
        grid=(bsz, seq // tm),
        in_specs=[row(D_MODEL), _mod_spec(mod), _const_spec((1, D_MODEL)),
                  _const_spec(w["wz"].shape), _const_spec(w["wx"].shape),
                  _const_spec(w["wdh"].shape), _const_spec(w["wdl"].shape),
                  _const_spec((1, LANES))],
        out_specs=[row(SSD_INNER), row(SSD_CONV_DIM), row(LANES)],
        out_shape=[jax.ShapeDtypeStruct((bsz, seq, SSD_INNER), BF16),
                   jax.ShapeDtypeStruct((bsz, seq, SSD_CONV_DIM), F32),
                   jax.ShapeDtypeStruct((bsz, seq, LANES), F32)],
        compiler_params=_cparams(("arbitrary", "arbitrary"), 48 << 20),
        name="ssd_in",
    )(x, mod, w["g"], w["wz"], w["wx"], w["wdh"], w["wdl"], w["db"])


def _conv_kernel(u_ref, w_ref, b_ref, o_ref, *, rows_per_step):
    seq, width = u_ref.shape
    half = SSD_CONV // 2
    ridx = lax.broadcasted_iota(jnp.int32, (rows_per_step, 1), 0)
    for r0 in range(0, seq, rows_per_step):
        acc = jnp.zeros((rows_per_step, width), F32) + b_ref[...]
        for j in range(SSD_CONV):
            lo = r0 + j - half
            if lo >= 0 and lo + rows_per_step <= seq:
                tap = u_ref[lo:lo + rows_per_step, :]
            else:
                blk = u_ref[r0:r0 + rows_per_step, :]
                sh = half - j
                tap = pltpu.roll(blk, sh % rows_per_step, 0)
                valid = (ridx >= sh) if sh > 0 else (ridx < rows_per_step + sh)
                tap = jnp.where(valid, tap, 0.0)
            acc = acc + tap * w_ref[j:j + 1, :]
        o_ref[r0:r0 + rows_per_step, :] = _silu(acc).astype(BF16)


def _conv(u, w, b, width=512, rows_per_step=256):
    bsz, seq, ch = u.shape
    return pl.pallas_call(
        functools.partial(_conv_kernel, rows_per_step=min(rows_per_step, seq)),
        grid=(bsz, ch // width),
        in_specs=[pl.BlockSpec((None, seq, width), lambda b_, c: (b_, 0, c)),
                  pl.BlockSpec((SSD_CONV, width), lambda b_, c: (0, c)),
                  pl.BlockSpec((1, width), lambda b_, c: (0, c))],
        out_specs=pl.BlockSpec((None, seq, width), lambda b_, c: (b_, 0, c)),
        out_shape=jax.ShapeDtypeStruct((bsz, seq, ch), BF16),
        compiler_params=_cparams(("arbitrary", "arbitrary"), 40 << 20),
        name="conv",
    )(u, w, b)


def _ssd_kernel(xc_ref, bc_ref, cc_ref, dtc_ref, xx_ref, bx_ref, cx_ref, dtx_ref,
                alog_ref, dsk_ref, y_ref, s_ref):
    g = pl.program_id(1)
    n_ctx = xc_ref.shape[0] // CHUNK
    n_lat = xx_ref.shape[0] // CHUNK
    a_all = jnp.broadcast_to(-jnp.exp(alog_ref[...]), (8, LANES))
    qi = lax.broadcasted_iota(jnp.int32, (CHUNK, CHUNK), 0)
    ki = lax.broadcasted_iota(jnp.int32, (CHUNK, CHUNK), 1)
    lane = lax.broadcasted_iota(jnp.int32, (1, LANES), 1)
    low_half = lane < SSD_HEADDIM
    n_pair = HEADS_PER_GROUP // 2

    def pair_cols(v, pr):
        return jnp.where(low_half, v[:, 2 * pr:2 * pr + 1], v[:, 2 * pr + 1:2 * pr + 2])

    def chunk_step(x_ref, b_ref, c_ref, dt_ref, r0, direction, emit):
        rows = pl.ds(r0, CHUNK)
        causal = (ki <= qi) if direction == 0 else (ki >= qi)
        shift = (LANES - (direction * SSD_HEADS + g * HEADS_PER_GROUP)) & (LANES - 1)
        dts = pltpu.roll(dt_ref[rows, :], shift, 1)
        a_sel = pltpu.roll(a_all, shift, 1)[0:1, :]
        a = dts * a_sel
        c = _dot_f32(causal.astype(F32), a)
        c_t = c.T
        dt_t = dts.T
        tot = c[CHUNK - 1:CHUNK, :] if direction == 0 else c[0:1, :]
        bmat = b_ref[rows, :]
        x = x_ref[rows, :]
        xf = x.astype(F32)
        to_end = jnp.exp(tot - c) * dts
        if emit:
            cmat = c_ref[rows, :]
            cb = _dot_nt(cmat, bmat)
            e_c = jnp.exp(c)
            y_off = _dot(cmat, s_ref[...].astype(BF16))
        ys, xs_scaled, cds = [], [], []
        cd = jnp.exp(tot)
        for pr in range(n_pair):
            ps = slice(pr * LANES, (pr + 1) * LANES)
            if emit:
                ws = []
                for hd in (2 * pr, 2 * pr + 1):
                    seg = c[:, hd:hd + 1] - c_t[hd:hd + 1, :]
                    dec = jnp.exp(jnp.where(causal, seg, -jnp.inf))
                    ws.append((cb * dec * dt_t[hd:hd + 1, :]).astype(BF16))
                w2 = jnp.concatenate(ws, axis=1)
                xp = x[:, ps]
                zero = jnp.zeros_like(xp)
                rhs = jnp.concatenate([jnp.where(low_half, xp, zero),
                                       jnp.where(low_half, zero, xp)], axis=0)
                ys.append(_dot(w2, rhs) + pair_cols(e_c, pr) * y_off[:, ps])
            xs_scaled.append((xf[:, ps] * pair_cols(to_end, pr)).astype(BF16))
            cds.append(pair_cols(cd, pr))
        b_t = bmat.astype(F32).T.astype(BF16)
        upd = _dot(b_t, jnp.concatenate(xs_scaled, axis=1))
        s_ref[...] = s_ref[...] * jnp.concatenate(cds, axis=1) + upd
        if emit:
            return jnp.concatenate(ys, axis=1), xf
        return None, None

    dsk = dsk_ref[...]
    for direction in range(2):
        s_ref[...] = jnp.zeros_like(s_ref)
        order = range(n_ctx) if direction == 0 else range(n_ctx - 1, -1, -1)
        for ci in order:
            chunk_step(xc_ref, bc_ref, cc_ref, dtc_ref, ci * CHUNK, direction, False)

        def body(i, carry, direction=direction):
            ci = i if direction == 0 else n_lat - 1 - i
            r0 = pl.multiple_of(ci * CHUNK, CHUNK)
            y, xf = chunk_step(xx_ref, bx_ref, cx_ref, dtx_ref, r0, direction, True)
            if direction == 0:
                y_ref[pl.ds(r0, CHUNK), :] = y + dsk * xf
            else:
                y_ref[pl.ds(r0, CHUNK), :] = y_ref[pl.ds(r0, CHUNK), :] + y
            return carry

        lax.fori_loop(0, n_lat, body, 0)


def _ssd(uc, dtc, ux, dtx, alog, dsk):
    bsz, lc, _ = uc.shape
    lx = ux.shape[1]
    b_blk0 = SSD_INNER // SSD_STATE
    c_blk0 = b_blk0 + SSD_GROUPS

    def seg_specs(rows):
        return [pl.BlockSpec((None, rows, GROUP_W), lambda b, g: (b, 0, g)),
                pl.BlockSpec((None, rows, SSD_STATE), lambda b, g: (b, 0, b_blk0 + g)),
                pl.BlockSpec((None, rows, SSD_STATE), lambda b, g: (b, 0, c_blk0 + g)),
                pl.BlockSpec((None, rows, LANES), lambda b, g: (b, 0, 0))]

    return pl.pallas_call(
        _ssd_kernel,
        grid=(bsz, SSD_GROUPS),
        in_specs=seg_specs(lc) + seg_specs(lx) + [
            pl.BlockSpec((1, LANES), lambda b, g: (0, 0)),
            pl.BlockSpec((1, GROUP_W), lambda b, g: (0, g))],
        out_specs=pl.BlockSpec((None, lx, GROUP_W), lambda b, g: (b, 0, g)),
        out_shape=jax.ShapeDtypeStruct((bsz, lx, SSD_INNER), F32),
        scratch_shapes=[pltpu.VMEM((SSD_STATE, GROUP_W), F32)],
        compiler_params=_cparams(("arbitrary", "arbitrary"), 48 << 20),
        name="ssd",
    )(uc, uc, uc, dtc, ux, ux, ux, dtx, alog, dsk)


def _rope_partner_index():
    j = np.arange(QK_ROPE)
    return np.where(j % 16 < 8, j + 8, j - 8)


def _rope_tables(seq):
    t = np.arange(seq)
    pos = np.stack([t // GRID_W, t % GRID_W], axis=1).astype(np.float32)
    j = np.arange(QK_ROPE)
    inv = ROPE_THETA ** (-(jnp.arange(8, dtype=F32)) * 2.0 / 16.0)
    ang = jnp.asarray(pos)[:, j // 16] * inv[j % 8][None, :]
    sign = jnp.asarray(np.where(j % 16 < 8, -1.0, 1.0).astype(np.float32))
    return jnp.cos(ang), jnp.sin(ang) * sign


def _mla_tables(seq, rotate):
    scale = 1.0 / math.sqrt(QK_NOPE + QK_ROPE)
    if rotate:
        cos, sin = _rope_tables(seq)
    else:
        cos, sin = jnp.ones((seq, QK_ROPE), F32), jnp.zeros((seq, QK_ROPE), F32)
    one = jnp.ones((seq, QK_NOPE), F32)
    zero = jnp.zeros((seq, QK_NOPE), F32)
    return {"q": jnp.concatenate([one, cos, sin], axis=1) * scale,
            "kc": jnp.concatenate([zero, cos, cos], axis=1),
            "ks": jnp.concatenate([zero, sin, sin], axis=1)}


def _mla_weights(gain, w_in, q_norm, w_uq, kv_norm, w_ukv):
    partner = _rope_partner_index()
    kr = w_in[:, Q_LORA + KV_LORA:]
    krp = kr[:, partner]
    zero = jnp.zeros((D_MODEL, QK_NOPE), F32)
    win = jnp.concatenate([w_in[:, :Q_LORA + KV_LORA], zero, kr, kr, zero, krp, krp], axis=1)
    uq = w_uq.reshape(Q_LORA, MLA_HEADS, QK_NOPE + QK_ROPE)
    rope = uq[:, :, QK_NOPE:]
    wq = jnp.concatenate([uq, rope[:, :, partner]], axis=2).reshape(Q_LORA, MLA_HEADS * HEAD_LANES)
    ukv = w_ukv.reshape(KV_LORA, MLA_HEADS, QK_NOPE + V_HEAD)
    wk = jnp.concatenate([ukv[:, :, :QK_NOPE], jnp.zeros((KV_LORA, MLA_HEADS, QK_NOPE), F32)], axis=2)
    return {"win": win.astype(BF16), "qg": q_norm.reshape(1, -1), "wq": wq.astype(BF16),
            "kvg": kv_norm.reshape(1, -1), "wk": wk.reshape(KV_LORA, -1).astype(BF16),
            "wv": ukv[:, :, QK_NOPE:].reshape(KV_LORA, -1).astype(BF16)}


def _post_weights(wo, gf, w1, w3, w2):
    return {"wo": wo.astype(BF16), "gf": gf.reshape(1, -1), "w1": w1.astype(BF16),
            "w3": w3.astype(BF16), "w2": w2.astype(BF16)}


def _ssd_in_weights(gain, w_in, dt_bias):
    wd = jnp.pad(w_in[:, SSD_INNER + SSD_CONV_DIM:], ((0, 0), (0, LANES - 2 * SSD_HEADS)))
    wdh = wd.astype(BF16)
    wdl = (wd - wdh.astype(F32)).astype(BF16)
    db = jnp.pad(dt_bias.reshape(1, -1), ((0, 0), (0, LANES - 2 * SSD_HEADS)))
    return {"g": gain.reshape(1, -1), "wz": w_in[:, :SSD_INNER].astype(BF16),
            "wx": w_in[:, SSD_INNER:SSD_INNER + SSD_CONV_DIM].astype(BF16),
            "wdh": wdh, "wdl": wdl, "db": db}


def kernel(x, c, ctx, c_ctx, ada_w, ada_b, norm_mix, norm_ffn, ffn_w1, ffn_w3, ffn_w2,
           mla_w_in, mla_q_norm, mla_w_uq, mla_kv_norm, mla_w_ukv, mla_w_o,
           ssd_w_in, ssd_conv_w, ssd_conv_b, ssd_dt_bias, ssd_a_log, ssd_d, ssd_norm, ssd_w_o,
           final_norm):
    bsz, seq, _ = x.shape
    lc = ctx.shape[1]
    tm = 256

    cc = jnp.zeros((8, D_MODEL), F32).at[:bsz].set(c).at[bsz].set(c_ctx)
    mod = _ada(cc, ada_w, ada_b)
    mod_x = [mod[l, :bsz].reshape(bsz, 6, D_MODEL) for l in range(DEPTH)]
    mod_c = [mod[l, bsz:bsz + 1].reshape(1, 6, D_MODEL) for l in range(DEPTH)]

    wm = _mla_weights(norm_mix[0], mla_w_in[0], mla_q_norm[0], mla_w_uq[0], mla_kv_norm[0], mla_w_ukv[0])
    g0 = norm_mix[0].reshape(1, -1)
    q_c, k_c, v_c = _mla_proj(ctx, mod_c[0], g0, wm, _mla_tables(lc, False), tm)
    q_x, k_x, v_x = _mla_proj(x, mod_x[0], g0, wm, _mla_tables(seq, True), tm)
    o_x = _attn(q_x, [(k_c, v_c), (k_x, v_x)], tq=256, tk=512)
    o_c = _attn(q_c, [(k_c, v_c)], tq=lc, tk=512)
    wp0 = _post_weights(mla_w_o[0], norm_ffn[0], ffn_w1[0], ffn_w3[0], ffn_w2[0])
    x = _post(x, o_x, mod_x[0], wp0, tm)
    ctx = _post(ctx, o_c, mod_c[0], wp0, tm)

    ws = _ssd_in_weights(norm_mix[1], ssd_w_in[0], ssd_dt_bias[0])
    _, xbc_c, dt_c = _ssd_in(ctx, mod_c[1], ws, tm)
    z_x, xbc_x, dt_x = _ssd_in(x, mod_x[1], ws, tm)
    u_c = _conv(xbc_c, ssd_conv_w[0], ssd_conv_b[0].reshape(1, -1))
    u_x = _conv(xbc_x, ssd_conv_w[0], ssd_conv_b[0].reshape(1, -1))
    alog = jnp.pad(ssd_a_log[0].reshape(1, -1), ((0, 0), (0, LANES - 2 * SSD_HEADS)))
    dsk = jnp.repeat(ssd_d[0], SSD_HEADDIM).reshape(1, -1)
    y = _ssd(u_c, dt_c, u_x, dt_x, alog, dsk)
    wp1 = _post_weights(ssd_w_o[0], norm_ffn[1], ffn_w1[1], ffn_w3[1], ffn_w2[1])
    return _post(x, (y, z_x, ssd_norm[0].reshape(1, -1)), mod_x[1], wp1, tm,
                 final_gain=final_norm.reshape(1, -1))
```

```python
import functools
import math

import numpy as np
import jax
import jax.numpy as jnp
from jax import lax
from jax.experimental import pallas as pl
from jax.experimental.pallas import tpu as pltpu

D_MODEL = 1024
DEPTH = 2
GRID_W = 64
EPS = 1e-6
MLA_HEADS = 16
Q_LORA = 512
KV_LORA = 256
QK_NOPE = 64
QK_ROPE = 32
V_HEAD = 64
ROPE_THETA = 10000.0
SSD_INNER = 2 * D_MODEL
SSD_HEADDIM = 64
SSD_HEADS = SSD_INNER // SSD_HEADDIM
SSD_GROUPS = 4
SSD_STATE = 128
SSD_CONV = 5
SSD_CONV_DIM = SSD_INNER + 2 * SSD_GROUPS * SSD_STATE
CHUNK = 128
FFN_HIDDEN = ((8 * D_MODEL // 3 + 255) // 256) * 256

LANES = 128
V7X_VMEM_BYTES = 64 * 1024 * 1024
VMEM_LIMIT_CAP = 56 * 1024 * 1024

F32 = jnp.float32
BF16 = jnp.bfloat16
HEAD_LANES = 2 * QK_NOPE
HEADS_PER_GROUP = SSD_HEADS // SSD_GROUPS
GROUP_W = HEADS_PER_GROUP * SSD_HEADDIM


def _cparams(semantics, vmem_bytes):
    return pltpu.CompilerParams(
        dimension_semantics=semantics,
        vmem_limit_bytes=int(min(VMEM_LIMIT_CAP, vmem_bytes)),
    )


def _const_spec(shape):
    nd = len(shape)
    return pl.BlockSpec(shape, lambda *_: (0,) * nd, pipeline_mode=pl.Buffered(1))


def _rms(x, g):
    return x * lax.rsqrt(jnp.mean(x * x, axis=-1, keepdims=True) + EPS) * g


def _silu(x):
    return x * (1.0 / (1.0 + jnp.exp(-x)))


def _dot(a, b):
    return jnp.dot(a, b, preferred_element_type=F32)


def _dot_nt(a, b):
    return lax.dot_general(a, b, (((1,), (1,)), ((), ())), preferred_element_type=F32)


def _dot_f32(a, b):
    return jnp.dot(a, b, preferred_element_type=F32, precision=lax.Precision.HIGHEST)


def _ada_kernel(c_ref, w_ref, b_ref, o_ref):
    s = _silu(c_ref[...])
    o_ref[...] = _dot_f32(s, w_ref[...]) + b_ref[...]


def _ada(cc, ada_w, ada_b):
    n_tiles = ada_w.shape[2] // D_MODEL
    return pl.pallas_call(
        _ada_kernel,
        grid=(DEPTH, n_tiles),
        in_specs=[
            pl.BlockSpec((8, D_MODEL), lambda l, j: (0, 0)),
            pl.BlockSpec((None, D_MODEL, D_MODEL), lambda l, j: (l, 0, j)),
            pl.BlockSpec((None, 1, D_MODEL), lambda l, j: (l, 0, j)),
        ],
        out_specs=pl.BlockSpec((None, 8, D_MODEL), lambda l, j: (l, 0, j)),
        out_shape=jax.ShapeDtypeStruct((DEPTH, 8, ada_w.shape[2]), F32),
        compiler_params=_cparams(("arbitrary", "arbitrary"), 24 << 20),
        name="ada",
    )(cc, ada_w, ada_b.reshape(DEPTH, 1, -1))


def _mod_spec(mod):
    if mod.shape[0] == 1:
        return pl.BlockSpec((None, 6, D_MODEL), lambda b, i: (0, 0, 0))
    return pl.BlockSpec((None, 6, D_MODEL), lambda b, i: (b, 0, 0))


def _mla_proj_kernel(x_ref, mod_ref, g_ref, win_ref, qg_ref, wq_ref, kvg_ref, wk_ref, wv_ref,
                     tq_ref, tkc_ref, tks_ref, q_ref, k_ref, v_ref):
    x = x_ref[...]
    h = _rms(x, g_ref[...]) * (1.0 + mod_ref[1:2, :]) + mod_ref[0:1, :]
    p = _dot(h.astype(BF16), win_ref[...])
    cq = p[:, :Q_LORA]
    ckv = p[:, Q_LORA:Q_LORA + KV_LORA]
    kr_a = p[:, Q_LORA + KV_LORA:Q_LORA + KV_LORA + LANES]
    kr_b = p[:, Q_LORA + KV_LORA + LANES:]
    cqn = _rms(cq, qg_ref[...]).astype(BF16)
    ckvn = _rms(ckv, kvg_ref[...]).astype(BF16)
    qf = _dot(cqn, wq_ref[...])
    kf = _dot(ckvn, wk_ref[...])
    kr = kr_a * tkc_ref[...] + kr_b * tks_ref[...]
    tq = tq_ref[...]
    for hd in range(MLA_HEADS):
        sl = slice(hd * HEAD_LANES, (hd + 1) * HEAD_LANES)
        q_ref[hd] = (qf[:, sl] * tq).astype(BF16)
        k_ref[hd] = (kf[:, sl] + kr).astype(BF16)
    v_ref[...] = _dot(ckvn, wv_ref[...]).astype(BF16)


def _mla_proj(x, mod, gain, w, tabs, tm):
    bsz, seq, _ = x.shape
    row = lambda n: pl.BlockSpec((None, tm, n), lambda b, i: (b, i, 0))
    heads = pl.BlockSpec((None, MLA_HEADS, tm, HEAD_LANES), lambda b, i: (b, 0, i, 0))
    tab = pl.BlockSpec((tm, LANES), lambda b, i: (i, 0))
    return pl.pallas_call(
        _mla_proj_kernel,
        grid=(bsz, seq // tm),
        in_specs=[
            row(D_MODEL), _mod_spec(mod), _const_spec((1, D_MODEL)),
            _const_spec(w["win"].shape), _const_spec((1, Q_LORA)), _const_spec(w["wq"].shape),
            _const_spec((1, KV_LORA)), _const_spec(w["wk"].shape), _const_spec(w["wv"].shape),
            tab, tab, tab,
        ],
        out_specs=[heads, heads, row(MLA_HEADS * V_HEAD)],
        out_shape=[
            jax.ShapeDtypeStruct((bsz, MLA_HEADS, seq, HEAD_LANES), BF16),
            jax.ShapeDtypeStruct((bsz, MLA_HEADS, seq, HEAD_LANES), BF16),
            jax.ShapeDtypeStruct((bsz, seq, MLA_HEADS * V_HEAD), BF16),
        ],
        compiler_params=_cparams(("arbitrary", "arbitrary"), 40 << 20),
        name="mla_proj",
    )(x, mod, gain, w["win"], w["qg"], w["wq"], w["kvg"], w["wk"], w["wv"],
      tabs["q"], tabs["kc"], tabs["ks"])


def _attn_kernel(*refs, n_kv, tq, tk):
    q_ref = refs[0]
    kv_refs = [(refs[1 + 2 * j], refs[2 + 2 * j]) for j in range(n_kv)]
    o_ref = refs[1 + 2 * n_kv]
    s_ref, m_ref = refs[2 + 2 * n_kv:]
    n_q = q_ref.shape[1] // tq
    lane = lax.broadcasted_iota(jnp.int32, (1, LANES), 1)
    chunks = []
    col = 0
    for k_ref, v_ref in kv_refs:
        rows = k_ref.shape[1]
        step = min(tk, rows)
        for r0 in range(0, rows, step):
            chunks.append((k_ref, v_ref, r0, step, col))
            col += step

    def tile_rows(i):
        start = i * tq
        return pl.ds(start if isinstance(start, int) else pl.multiple_of(start, tq), tq)

    def scores(i, hh):
        qh = q_ref[hh, tile_rows(i), :]
        m = jnp.full((tq, 1), -jnp.inf, F32)
        for k_ref, _, r0, rows, c0 in chunks:
            s = _dot_nt(qh, k_ref[hh, r0:r0 + rows, :])
            s_ref[hh, :, c0:c0 + rows] = s
            m = jnp.maximum(m, jnp.max(s, axis=-1, keepdims=True))
        m_ref[hh] = m

    def softmax_pv(hh):
        vmask = (lane < V_HEAD) if hh == 0 else (lane >= V_HEAD)
        m = m_ref[hh]
        l = jnp.zeros((tq, 1), F32)
        acc = jnp.zeros((tq, LANES), F32)
        for _, v_ref, r0, rows, c0 in chunks:
            p = jnp.exp(s_ref[hh, :, c0:c0 + rows] - m)
            l = l + jnp.sum(p, axis=-1, keepdims=True)
            vv = v_ref[r0:r0 + rows, :]
            vv = jnp.where(vmask, vv, jnp.zeros_like(vv))
            acc = acc + _dot(p.astype(BF16), vv)
        return acc / l

    def tile(i, next_scores):
        scores(i, 1)
        out = softmax_pv(0)
        if next_scores:
            scores(i + 1, 0)
        out = out + softmax_pv(1)
        o_ref[tile_rows(i), :] = out.astype(BF16)

    scores(0, 0)

    def body(i, carry):
        tile(i, True)
        return carry

    lax.fori_loop(0, n_q - 1, body, 0)
    tile(n_q - 1, False)


def _attn(q, kvs, tq, tk):
    bsz, _, lq, _ = q.shape
    n_pairs = MLA_HEADS // 2
    total = sum(k.shape[2] for k, _ in kvs)
    in_specs = [pl.BlockSpec((None, 2, lq, HEAD_LANES), lambda b, h: (b, h, 0, 0))]
    args = [q]
    for k, v in kvs:
        t = k.shape[2]
        in_specs.append(pl.BlockSpec((None, 2, t, HEAD_LANES), lambda b, h: (b, h, 0, 0)))
        in_specs.append(pl.BlockSpec((None, t, 2 * V_HEAD), lambda b, h: (b, 0, h)))
        args += [k, v]
    blocks = 2 * (2 * lq * HEAD_LANES + total * (2 * HEAD_LANES + 2 * V_HEAD) + lq * 2 * V_HEAD) * 2
    vmem = blocks + 2 * tq * (total + LANES) * 4 + (16 << 20)
    return pl.pallas_call(
        functools.partial(_attn_kernel, n_kv=len(kvs), tq=tq, tk=tk),
        grid=(bsz, n_pairs),
        in_specs=in_specs,
        out_specs=pl.BlockSpec((None, lq, 2 * V_HEAD), lambda b, h: (b, 0, h)),
        out_shape=jax.ShapeDtypeStruct((bsz, lq, MLA_HEADS * V_HEAD), BF16),
        scratch_shapes=[pltpu.VMEM((2, tq, total), F32), pltpu.VMEM((2, tq, 1), F32)],
        compiler_params=_cparams(("arbitrary", "arbitrary"), vmem),
        name="attn",
    )(*args)


def _post_kernel(*refs, ssd, final):
    it = iter(refs)
    x_ref = next(it)
    if ssd:
        y_ref, z_ref, ng_ref = next(it), next(it), next(it)
    else:
        a_ref = next(it)
    mod_ref, wo_ref, gf_ref, w1_ref, w3_ref, w2_ref = (next(it) for _ in range(6))
    fg_ref = next(it) if final else None
    o_ref = next(it)

    if ssd:
        gated = y_ref[...] * _silu(z_ref[...].astype(F32))
        a = _rms(gated, ng_ref[...]).astype(BF16)
    else:
        a = a_ref[...]
    x1 = x_ref[...] + mod_ref[2:3, :] * _dot(a, wo_ref[...])
    hx = (_rms(x1, gf_ref[...]) * (1.0 + mod_ref[4:5, :]) + mod_ref[3:4, :]).astype(BF16)
    u = _dot(hx, w1_ref[...])
    w = _dot(hx, w3_ref[...])
    t = (_silu(u) * w).astype(BF16)
    x2 = x1 + mod_ref[5:6, :] * _dot(t, w2_ref[...])
    if final:
        x2 = _rms(x2, fg_ref[...])
    o_ref[...] = x2


def _post(x, mix_in, mod, w, tm, final_gain=None):
    bsz, seq, _ = x.shape
    ssd = isinstance(mix_in, tuple)
    row = lambda n: pl.BlockSpec((None, tm, n), lambda b, i: (b, i, 0))
    in_specs = [row(D_MODEL)]
    args = [x]
    if ssd:
        y, z, ng = mix_in
        in_specs += [row(SSD_INNER), row(SSD_INNER), _const_spec((1, SSD_INNER))]
        args += [y, z, ng]
    else:
        in_specs.append(row(mix_in.shape[2]))
        args.append(mix_in)
    in_specs += [_mod_spec(mod), _const_spec(w["wo"].shape), _const_spec((1, D_MODEL)),
                 _const_spec(w["w1"].shape), _const_spec(w["w3"].shape), _const_spec(w["w2"].shape)]
    args += [mod, w["wo"], w["gf"], w["w1"], w["w3"], w["w2"]]
    if final_gain is not None:
        in_specs.append(_const_spec((1, D_MODEL)))
        args.append(final_gain)
    return pl.pallas_call(
        functools.partial(_post_kernel, ssd=ssd, final=final_gain is not None),
        grid=(bsz, seq // tm),
        in_specs=in_specs,
        out_specs=row(D_MODEL),
        out_shape=jax.ShapeDtypeStruct((bsz, seq, D_MODEL), F32),
        compiler_params=_cparams(("arbitrary", "arbitrary"), VMEM_LIMIT_CAP),
        name="post_ssd" if ssd else "post_mla",
    )(*args)


def _ssd_in_kernel(x_ref, mod_ref, g_ref, wz_ref, wx_ref, wdh_ref, wdl_ref, db_ref,
                   z_ref, xbc_ref, dt_ref):
    h = _rms(x_ref[...], g_ref[...]) * (1.0 + mod_ref[1:2, :]) + mod_ref[0:1, :]
    hb = h.astype(BF16)
    z_ref[...] = _dot(hb, wz_ref[...]).astype(BF16)
    xbc_ref[...] = _dot(hb, wx_ref[...])
    hl = (h - hb.astype(F32)).astype(BF16)
    raw = _dot(hb, wdh_ref[...]) + _dot(hl, wdh_ref[...]) + _dot(hb, wdl_ref[...])
    raw = raw + db_ref[...]
    dt_ref[...] = jnp.maximum(raw, 0.0) + jnp.log(1.0 + jnp.exp(-jnp.abs(raw)))


def _ssd_in(x, mod, w, tm):
    bsz, seq, _ = x.shape
    row = lambda n: pl.BlockSpec((None, tm, n), lambda b, i: (b, i, 0))
    return pl.pallas_call(
        _ssd_in_kernel,
        grid=(bsz, seq // tm),
        in_specs=[row(D_MODEL), _mod_spec(mod), _const_spec((1, D_MODEL)),
                  _const_spec(w["wz"].shape), _const_spec(w["wx"].shape),
                  _const_spec(w["wdh"].shape), _const_spec(w["wdl"].shape),
                  _const_spec((1, LANES))],
        out_specs=[row(SSD_INNER), row(SSD_CONV_DIM), row(LANES)],
        out_shape=[jax.ShapeDtypeStruct((bsz, seq, SSD_INNER), BF16),
                   jax.ShapeDtypeStruct((bsz, seq, SSD_CONV_DIM), F32),
                   jax.ShapeDtypeStruct((bsz, seq, LANES), F32)],
        compiler_params=_cparams(("arbitrary", "arbitrary"), 48 << 20),
        name="ssd_in",
    )(x, mod, w["g"], w["wz"], w["wx"], w["wdh"], w["wdl"], w["db"])


def _conv_kernel(u_ref, w_ref, b_ref, o_ref, *, rows_per_step):
    seq, width = u_ref.shape
    half = SSD_CONV // 2
    ridx = lax.broadcasted_iota(jnp.int32, (rows_per_step, 1), 0)
    for r0 in range(0, seq, rows_per_step):
        acc = jnp.zeros((rows_per_step, width), F32) + b_ref[...]
        for j in range(SSD_CONV):
            lo = r0 + j - half
            if lo >= 0 and lo + rows_per_step <= seq:
                tap = u_ref[lo:lo + rows_per_step, :]
            else:
                blk = u_ref[r0:r0 + rows_per_step, :]
                sh = half - j
                tap = pltpu.roll(blk, sh % rows_per_step, 0)
                valid = (ridx >= sh) if sh > 0 else (ridx < rows_per_step + sh)
                tap = jnp.where(valid, tap, 0.0)
            acc = acc + tap * w_ref[j:j + 1, :]
        o_ref[r0:r0 + rows_per_step, :] = _silu(acc).astype(BF16)


def _conv(u, w, b, width=512, rows_per_step=256):
    bsz, seq, ch = u.shape
    return pl.pallas_call(
        functools.partial(_conv_kernel, rows_per_step=min(rows_per_step, seq)),
        grid=(bsz, ch // width),
        in_specs=[pl.BlockSpec((None, seq, width), lambda b_, c: (b_, 0, c)),
                  pl.BlockSpec((SSD_CONV, width), lambda b_, c: (0, c)),
                  pl.BlockSpec((1, width), lambda b_, c: (0, c))],
        out_specs=pl.BlockSpec((None, seq, width), lambda b_, c: (b_, 0, c)),
        out_shape=jax.ShapeDtypeStruct((bsz, seq, ch), BF16),
        compiler_params=_cparams(("arbitrary", "arbitrary"), 40 << 20),
        name="conv",
    )(u, w, b)


def _ssd_kernel(xc_ref, bc_ref, cc_ref, dtc_ref, xx_ref, bx_ref, cx_ref, dtx_ref,
                alog_ref, dsk_ref, y_ref, s_ref):
    g = pl.program_id(1)
    n_ctx = xc_ref.shape[0] // CHUNK
    n_lat = xx_ref.shape[0] // CHUNK
    a_all = jnp.broadcast_to(-jnp.exp(alog_ref[...]), (8, LANES))
    qi = lax.broadcasted_iota(jnp.int32, (CHUNK, CHUNK), 0)
    ki = lax.broadcasted_iota(jnp.int32, (CHUNK, CHUNK), 1)
    lane = lax.broadcasted_iota(jnp.int32, (1, LANES), 1)
    low_half = lane < SSD_HEADDIM
    n_pair = HEADS_PER_GROUP // 2

    def pair_cols(v, pr):
        return jnp.where(low_half, v[:, 2 * pr:2 * pr + 1], v[:, 2 * pr + 1:2 * pr + 2])

    def split3(v):
        hi = v.astype(BF16)
        r1 = v - hi.astype(F32)
        mid = r1.astype(BF16)
        return hi, mid, (r1 - mid.astype(F32)).astype(BF16)

    def lane_bcast(v, col):
        return jnp.broadcast_to(v[:, col:col + 1], (v.shape[0], LANES))

    def chunk_step(x_ref, b_ref, c_ref, dt_ref, r0, direction, emit):
        rows = pl.ds(r0, CHUNK)
        st_ref = s_ref.at[direction]
        causal = (ki <= qi) if direction == 0 else (ki >= qi)
        shift = (LANES - (direction * SSD_HEADS + g * HEADS_PER_GROUP)) & (LANES - 1)
        dts = pltpu.roll(dt_ref[rows, :], shift, 1)
        a_sel = pltpu.roll(a_all, shift, 1)[0:1, :]
        a = dts * a_sel
        tri = jnp.where(causal, 1.0, 0.0).astype(BF16)
        c = sum(_dot(tri, piece) for piece in split3(a))
        c_t = c.T
        dt_t = dts.T
        tot = c[CHUNK - 1:CHUNK, :] if direction == 0 else c[0:1, :]
        bmat = b_ref[rows, :]
        x = x_ref[rows, :]
        xf = x.astype(F32)
        to_end = jnp.exp(tot - c) * dts
        cd = jnp.exp(tot)
        if emit:
            cmat = c_ref[rows, :]
            cb = _dot_nt(cmat, bmat)
            y_off = _dot(cmat, st_ref[...].astype(BF16))
        ys, xs_scaled, cds = [], [], []
        for pr in range(n_pair):
            ps = slice(pr * LANES, (pr + 1) * LANES)
            te_b = [lane_bcast(to_end, hd) for hd in (2 * pr, 2 * pr + 1)]
            if emit:
                ws, e_b = [], []
                for hd in (2 * pr, 2 * pr + 1):
                    cq = lane_bcast(c, hd)
                    dec = jnp.exp(jnp.where(causal, cq - c_t[hd:hd + 1, :], -jnp.inf))
                    ws.append((cb * dec * dt_t[hd:hd + 1, :]).astype(BF16))
                    e_b.append(jnp.exp(cq))
                w2 = jnp.concatenate(ws, axis=1)
                xp = x[:, ps]
                zero = jnp.zeros_like(xp)
                rhs = jnp.concatenate([jnp.where(low_half, xp, zero),
                                       jnp.where(low_half, zero, xp)], axis=0)
                ys.append(_dot(w2, rhs) + jnp.where(low_half, e_b[0], e_b[1]) * y_off[:, ps])
            xs_scaled.append((xf[:, ps] * jnp.where(low_half, te_b[0], te_b[1])).astype(BF16))
            cds.append(pair_cols(cd, pr))
        b_t = bmat.astype(F32).T.astype(BF16)
        upd = _dot(b_t, jnp.concatenate(xs_scaled, axis=1))
        st_ref[...] = st_ref[...] * jnp.concatenate(cds, axis=1) + upd
        if emit:
            return jnp.concatenate(ys, axis=1), xf
        return None, None

    dsk = dsk_ref[...]
    s_ref[...] = jnp.zeros_like(s_ref)
    for ci in range(n_ctx):
        chunk_step(xc_ref, bc_ref, cc_ref, dtc_ref, ci * CHUNK, 0, False)
        chunk_step(xc_ref, bc_ref, cc_ref, dtc_ref, (n_ctx - 1 - ci) * CHUNK, 1, False)

    def body(i, carry, first_touch):
        rf = pl.multiple_of(i * CHUNK, CHUNK)
        rb = pl.multiple_of((n_lat - 1 - i) * CHUNK, CHUNK)
        yf, xf = chunk_step(xx_ref, bx_ref, cx_ref, dtx_ref, rf, 0, True)
        yb, _ = chunk_step(xx_ref, bx_ref, cx_ref, dtx_ref, rb, 1, True)
        yf = yf + dsk * xf
        if first_touch:
            y_ref[pl.ds(rf, CHUNK), :] = yf
            y_ref[pl.ds(rb, CHUNK), :] = yb
        else:
            y_ref[pl.ds(rf, CHUNK), :] = y_ref[pl.ds(rf, CHUNK), :] + yf
            y_ref[pl.ds(rb, CHUNK), :] = y_ref[pl.ds(rb, CHUNK), :] + yb
        return carry

    assert n_lat % 2 == 0
    lax.fori_loop(0, n_lat // 2, functools.partial(body, first_touch=True), 0, unroll=2)
    lax.fori_loop(n_lat // 2, n_lat, functools.partial(body, first_touch=False), 0, unroll=2)


def _ssd(uc, dtc, ux, dtx, alog, dsk):
    bsz, lc, _ = uc.shape
    lx = ux.shape[1]
    b_blk0 = SSD_INNER // SSD_STATE
    c_blk0 = b_blk0 + SSD_GROUPS

    def seg_specs(rows):
        return [pl.BlockSpec((None, rows, GROUP_W), lambda b, g: (b, 0, g)),
                pl.BlockSpec((None, rows, SSD_STATE), lambda b, g: (b, 0, b_blk0 + g)),
                pl.BlockSpec((None, rows, SSD_STATE), lambda b, g: (b, 0, c_blk0 + g)),
                pl.BlockSpec((None, rows, LANES), lambda b, g: (b, 0, 0))]

    return pl.pallas_call(
        _ssd_kernel,
        grid=(bsz, SSD_GROUPS),
        in_specs=seg_specs(lc) + seg_specs(lx) + [
            pl.BlockSpec((1, LANES), lambda b, g: (0, 0)),
            pl.BlockSpec((1, GROUP_W), lambda b, g: (0, g))],
        out_specs=pl.BlockSpec((None, lx, GROUP_W), lambda b, g: (b, 0, g)),
        out_shape=jax.ShapeDtypeStruct((bsz, lx, SSD_INNER), F32),
        scratch_shapes=[pltpu.VMEM((2, SSD_STATE, GROUP_W), F32)],
        compiler_params=_cparams(("arbitrary", "arbitrary"), 48 << 20),
        name="ssd",
    )(uc, uc, uc, dtc, ux, ux, ux, dtx, alog, dsk)


def _rope_partner_index():
    j = np.arange(QK_ROPE)
    return np.where(j % 16 < 8, j + 8, j - 8)


def _rope_tables(seq):
    t = np.arange(seq)
    pos = np.stack([t // GRID_W, t % GRID_W], axis=1).astype(np.float32)
    j = np.arange(QK_ROPE)
    inv = ROPE_THETA ** (-(jnp.arange(8, dtype=F32)) * 2.0 / 16.0)
    ang = jnp.asarray(pos)[:, j // 16] * inv[j % 8][None, :]
    sign = jnp.asarray(np.where(j % 16 < 8, -1.0, 1.0).astype(np.float32))
    return jnp.cos(ang), jnp.sin(ang) * sign


def _mla_tables(seq, rotate):
    scale = 1.0 / math.sqrt(QK_NOPE + QK_ROPE)
    if rotate:
        cos, sin = _rope_tables(seq)
    else:
        cos, sin = jnp.ones((seq, QK_ROPE), F32), jnp.zeros((seq, QK_ROPE), F32)
    one = jnp.ones((seq, QK_NOPE), F32)
    zero = jnp.zeros((seq, QK_NOPE), F32)
    return {"q": jnp.concatenate([one, cos, sin], axis=1) * scale,
            "kc": jnp.concatenate([zero, cos, cos], axis=1),
            "ks": jnp.concatenate([zero, sin, sin], axis=1)}


def _mla_weights(w_in, q_norm, w_uq, kv_norm, w_ukv):
    partner = _rope_partner_index()
    kr = w_in[:, Q_LORA + KV_LORA:]
    krp = kr[:, partner]
    zero = jnp.zeros((D_MODEL, QK_NOPE), F32)
    win = jnp.concatenate([w_in[:, :Q_LORA + KV_LORA], zero, kr, kr, zero, krp, krp], axis=1)
    uq = w_uq.reshape(Q_LORA, MLA_HEADS, QK_NOPE + QK_ROPE)
    rope = uq[:, :, QK_NOPE:]
    wq = jnp.concatenate([uq, rope[:, :, partner]], axis=2).reshape(Q_LORA, MLA_HEADS * HEAD_LANES)
    ukv = w_ukv.reshape(KV_LORA, MLA_HEADS, QK_NOPE + V_HEAD)
    wk = jnp.concatenate([ukv[:, :, :QK_NOPE], jnp.zeros((KV_LORA, MLA_HEADS, QK_NOPE), F32)], axis=2)
    return {"win": win.astype(BF16), "qg": q_norm.reshape(1, -1), "wq": wq.astype(BF16),
            "kvg": kv_norm.reshape(1, -1), "wk": wk.reshape(KV_LORA, -1).astype(BF16),
            "wv": ukv[:, :, QK_NOPE:].reshape(KV_LORA, -1).astype(BF16)}


def _post_weights(wo, gf, w1, w3, w2):
    return {"wo": wo.astype(BF16), "gf": gf.reshape(1, -1), "w1": w1.astype(BF16),
            "w3": w3.astype(BF16), "w2": w2.astype(BF16)}


def _ssd_in_weights(gain, w_in, dt_bias):
    wd = jnp.pad(w_in[:, SSD_INNER + SSD_CONV_DIM:], ((0, 0), (0, LANES - 2 * SSD_HEADS)))
    wdh = wd.astype(BF16)
    wdl = (wd - wdh.astype(F32)).astype(BF16)
    db = jnp.pad(dt_bias.reshape(1, -1), ((0, 0), (0, LANES - 2 * SSD_HEADS)))
    return {"g": gain.reshape(1, -1), "wz": w_in[:, :SSD_INNER].astype(BF16),
            "wx": w_in[:, SSD_INNER:SSD_INNER + SSD_CONV_DIM].astype(BF16),
            "wdh": wdh, "wdl": wdl, "db": db}


def kernel(x, c, ctx, c_ctx, ada_w, ada_b, norm_mix, norm_ffn, ffn_w1, ffn_w3, ffn_w2,
           mla_w_in, mla_q_norm, mla_w_uq, mla_kv_norm, mla_w_ukv, mla_w_o,
           ssd_w_in, ssd_conv_w, ssd_conv_b, ssd_dt_bias, ssd_a_log, ssd_d, ssd_norm, ssd_w_o,
           final_norm):
    bsz, seq, _ = x.shape
    lc = ctx.shape[1]
    tm = 256

    cc = jnp.zeros((8, D_MODEL), F32).at[:bsz].set(c).at[bsz].set(c_ctx)
    mod = _ada(cc, ada_w, ada_b)
    mod_x = [mod[l, :bsz].reshape(bsz, 6, D_MODEL) for l in range(DEPTH)]
    mod_c = [mod[l, bsz:bsz + 1].reshape(1, 6, D_MODEL) for l in range(DEPTH)]

    wm = _mla_weights(mla_w_in[0], mla_q_norm[0], mla_w_uq[0], mla_kv_norm[0], mla_w_ukv[0])
    g0 = norm_mix[0].reshape(1, -1)
    q_c, k_c, v_c = _mla_proj(ctx, mod_c[0], g0, wm, _mla_tables(lc, False), tm)
    q_x, k_x, v_x = _mla_proj(x, mod_x[0], g0, wm, _mla_tables(seq, True), tm)
    o_x = _attn(q_x, [(k_c, v_c), (k_x, v_x)], tq=256, tk=512)
    o_c = _attn(q_c, [(k_c, v_c)], tq=lc, tk=512)
    wp0 = _post_weights(mla_w_o[0], norm_ffn[0], ffn_w1[0], ffn_w3[0], ffn_w2[0])
    x = _post(x, o_x, mod_x[0], wp0, tm)
    ctx = _post(ctx, o_c, mod_c[0], wp0, tm)

    ws = _ssd_in_weights(norm_mix[1], ssd_w_in[0], ssd_dt_bias[0])
    _, xbc_c, dt_c = _ssd_in(ctx, mod_c[1], ws, tm)
    z_x, xbc_x, dt_x = _ssd_in(x, mod_x[1], ws, tm)
    u_c = _conv(xbc_c, ssd_conv_w[0], ssd_conv_b[0].reshape(1, -1))
    u_x = _conv(xbc_x, ssd_conv_w[0], ssd_conv_b[0].reshape(1, -1))
    alog = jnp.pad(ssd_a_log[0].reshape(1, -1), ((0, 0), (0, LANES - 2 * SSD_HEADS)))
    dsk = jnp.repeat(ssd_d[0], SSD_HEADDIM).reshape(1, -1)
    y = _ssd(u_c, dt_c, u_x, dt_x, alog, dsk)
    wp1 = _post_weights(ssd_w_o[0], norm_ffn[1], ffn_w1[1], ffn_w3[1], ffn_w2[1])
    return _post(x, (y, z_x, ssd_norm[0].reshape(1, -1)), mod_x[1], wp1, tm,
                 final_gain=final_norm.reshape(1, -1))
```

```python
import functools
import math

import numpy as np
import jax
import jax.numpy as jnp
from jax import lax
from jax.experimental import pallas as pl
from jax.experimental.pallas import tpu as pltpu

D_MODEL = 1024
DEPTH = 2
GRID_W = 64
EPS = 1e-6
MLA_HEADS = 16
Q_LORA = 512
KV_LORA = 256
QK_NOPE = 64
QK_ROPE = 32
V_HEAD = 64
ROPE_THETA = 10000.0
SSD_INNER = 2 * D_MODEL
SSD_HEADDIM = 64
SSD_HEADS = SSD_INNER // SSD_HEADDIM
SSD_GROUPS = 4
SSD_STATE = 128
SSD_CONV = 5
SSD_CONV_DIM = SSD_INNER + 2 * SSD_GROUPS * SSD_STATE
CHUNK = 128
FFN_HIDDEN = ((8 * D_MODEL // 3 + 255) // 256) * 256

LANES = 128
V7X_VMEM_BYTES = 64 * 1024 * 1024
VMEM_LIMIT_CAP = 56 * 1024 * 1024

F32 = jnp.float32
BF16 = jnp.bfloat16
HEAD_LANES = 2 * QK_NOPE
HEADS_PER_GROUP = SSD_HEADS // SSD_GROUPS
GROUP_W = HEADS_PER_GROUP * SSD_HEADDIM


def _cparams(semantics, vmem_bytes):
    return pltpu.CompilerParams(
        dimension_semantics=semantics,
        vmem_limit_bytes=int(min(VMEM_LIMIT_CAP, vmem_bytes)),
    )


def _const_spec(shape):
    nd = len(shape)
    return pl.BlockSpec(shape, lambda *_: (0,) * nd, pipeline_mode=pl.Buffered(1))


def _rms(x, g):
    return x * lax.rsqrt(jnp.mean(x * x, axis=-1, keepdims=True) + EPS) * g


def _silu(x):
    return x * (1.0 / (1.0 + jnp.exp(-x)))


def _dot(a, b):
    return jnp.dot(a, b, preferred_element_type=F32)


def _dot_nt(a, b):
    return lax.dot_general(a, b, (((1,), (1,)), ((), ())), preferred_element_type=F32)


def _dot_f32(a, b):
    return jnp.dot(a, b, preferred_element_type=F32, precision=lax.Precision.HIGHEST)


def _ada_kernel(c_ref, w_ref, b_ref, o_ref):
    s = _silu(c_ref[...])
    o_ref[...] = _dot_f32(s, w_ref[...]) + b_ref[...]


def _ada(cc, ada_w, ada_b):
    n_tiles = ada_w.shape[2] // D_MODEL
    return pl.pallas_call(
        _ada_kernel,
        grid=(DEPTH, n_tiles),
        in_specs=[
            pl.BlockSpec((8, D_MODEL), lambda l, j: (0, 0)),
            pl.BlockSpec((None, D_MODEL, D_MODEL), lambda l, j: (l, 0, j)),
            pl.BlockSpec((None, 1, D_MODEL), lambda l, j: (l, 0, j)),
        ],
        out_specs=pl.BlockSpec((None, 8, D_MODEL), lambda l, j: (l, 0, j)),
        out_shape=jax.ShapeDtypeStruct((DEPTH, 8, ada_w.shape[2]), F32),
        compiler_params=_cparams(("arbitrary", "arbitrary"), 24 << 20),
        name="ada",
    )(cc, ada_w, ada_b.reshape(DEPTH, 1, -1))


def _mod_spec(mod):
    if mod.shape[0] == 1:
        return pl.BlockSpec((None, 6, D_MODEL), lambda b, i: (0, 0, 0))
    return pl.BlockSpec((None, 6, D_MODEL), lambda b, i: (b, 0, 0))


def _mla_proj_kernel(x_ref, mod_ref, g_ref, win_ref, qg_ref, wq_ref, kvg_ref, wk_ref, wv_ref,
                     tq_ref, tkc_ref, tks_ref, q_ref, k_ref, v_ref):
    x = x_ref[...]
    h = _rms(x, g_ref[...]) * (1.0 + mod_ref[1:2, :]) + mod_ref[0:1, :]
    p = _dot(h.astype(BF16), win_ref[...])
    cq = p[:, :Q_LORA]
    ckv = p[:, Q_LORA:Q_LORA + KV_LORA]
    kr_a = p[:, Q_LORA + KV_LORA:Q_LORA + KV_LORA + LANES]
    kr_b = p[:, Q_LORA + KV_LORA + LANES:]
    cqn = _rms(cq, qg_ref[...]).astype(BF16)
    ckvn = _rms(ckv, kvg_ref[...]).astype(BF16)
    qf = _dot(cqn, wq_ref[...])
    kf = _dot(ckvn, wk_ref[...])
    kr = kr_a * tkc_ref[...] + kr_b * tks_ref[...]
    tq = tq_ref[...]
    for hd in range(MLA_HEADS):
        sl = slice(hd * HEAD_LANES, (hd + 1) * HEAD_LANES)
        q_ref[hd] = (qf[:, sl] * tq).astype(BF16)
        k_ref[hd] = (kf[:, sl] + kr).astype(BF16)
    v_ref[...] = _dot(ckvn, wv_ref[...]).astype(BF16)


def _mla_proj(x, mod, gain, w, tabs, tm):
    bsz, seq, _ = x.shape
    row = lambda n: pl.BlockSpec((None, tm, n), lambda b, i: (b, i, 0))
    heads = pl.BlockSpec((None, MLA_HEADS, tm, HEAD_LANES), lambda b, i: (b, 0, i, 0))
    tab = pl.BlockSpec((tm, LANES), lambda b, i: (i, 0))
    return pl.pallas_call(
        _mla_proj_kernel,
        grid=(bsz, seq // tm),
        in_specs=[
            row(D_MODEL), _mod_spec(mod), _const_spec((1, D_MODEL)),
            _const_spec(w["win"].shape), _const_spec((1, Q_LORA)), _const_spec(w["wq"].shape),
            _const_spec((1, KV_LORA)), _const_spec(w["wk"].shape), _const_spec(w["wv"].shape),
            tab, tab, tab,
        ],
        out_specs=[heads, heads, row(MLA_HEADS * V_HEAD)],
        out_shape=[
            jax.ShapeDtypeStruct((bsz, MLA_HEADS, seq, HEAD_LANES), BF16),
            jax.ShapeDtypeStruct((bsz, MLA_HEADS, seq, HEAD_LANES), BF16),
            jax.ShapeDtypeStruct((bsz, seq, MLA_HEADS * V_HEAD), BF16),
        ],
        compiler_params=_cparams(("arbitrary", "arbitrary"), 40 << 20),
        name="mla_proj",
    )(x, mod, gain, w["win"], w["qg"], w["wq"], w["kvg"], w["wk"], w["wv"],
      tabs["q"], tabs["kc"], tabs["ks"])


def _attn_kernel(*refs, n_kv, tq, tk):
    q_ref = refs[0]
    kv_refs = [(refs[1 + 2 * j], refs[2 + 2 * j]) for j in range(n_kv)]
    o_ref = refs[1 + 2 * n_kv]
    s_ref, m_ref, vm_ref = refs[2 + 2 * n_kv:]
    n_q = q_ref.shape[1] // tq
    lane = lax.broadcasted_iota(jnp.int32, (1, LANES), 1)
    low_half = lane < V_HEAD
    chunks = []
    col = 0
    for k_ref, v_ref in kv_refs:
        rows = k_ref.shape[1]
        v = v_ref[...]
        one = jnp.ones_like(v)
        vm_ref[0, col:col + rows, :] = jnp.where(low_half, v, one)
        vm_ref[1, col:col + rows, :] = jnp.where(low_half, one, v)
        step = min(tk, rows)
        for r0 in range(0, rows, step):
            chunks.append((k_ref, r0, step, col + r0))
        col += rows

    def tile_rows(i):
        start = i * tq
        return pl.ds(start if isinstance(start, int) else pl.multiple_of(start, tq), tq)

    def scores(i, hh):
        qh = q_ref[hh, tile_rows(i), :]
        m_lanes = None
        for k_ref, r0, rows, c0 in chunks:
            s = _dot_nt(qh, k_ref[hh, r0:r0 + rows, :])
            s_ref[hh, :, c0:c0 + rows] = s
            blocks = [s[:, j:j + LANES] for j in range(0, rows, LANES)]
            cm = functools.reduce(jnp.maximum, blocks)
            m_lanes = cm if m_lanes is None else jnp.maximum(m_lanes, cm)
        m = jnp.max(m_lanes, axis=-1, keepdims=True)
        m_ref[hh] = jnp.broadcast_to(m, (tq, LANES))

    def softmax_pv(hh):
        acc = jnp.zeros((tq, LANES), F32)
        for _, _, rows, c0 in chunks:
            ps = [jnp.exp2(s_ref[hh, :, c0 + j:c0 + j + LANES] - m_ref[hh]).astype(BF16)
                  for j in range(0, rows, LANES)]
            acc = acc + _dot(jnp.concatenate(ps, axis=1), vm_ref[hh, c0:c0 + rows, :])
        return acc

    def tile(i, next_scores):
        scores(i, 1)
        acc0 = softmax_pv(0)
        if next_scores:
            scores(i + 1, 0)
        acc1 = softmax_pv(1)
        out = jnp.where(low_half, acc0 / pltpu.roll(acc0, V_HEAD, 1), acc1 / pltpu.roll(acc1, V_HEAD, 1))
        o_ref[tile_rows(i), :] = out.astype(BF16)

    scores(0, 0)

    def body(i, carry):
        tile(i, True)
        return carry

    lax.fori_loop(0, n_q - 1, body, 0)
    tile(n_q - 1, False)


def _attn(q, kvs, tq, tk):
    bsz, _, lq, _ = q.shape
    n_pairs = MLA_HEADS // 2
    total = sum(k.shape[2] for k, _ in kvs)
    in_specs = [pl.BlockSpec((None, 2, lq, HEAD_LANES), lambda b, h: (b, h, 0, 0))]
    args = [q]
    for k, v in kvs:
        t = k.shape[2]
        in_specs.append(pl.BlockSpec((None, 2, t, HEAD_LANES), lambda b, h: (b, h, 0, 0)))
        in_specs.append(pl.BlockSpec((None, t, 2 * V_HEAD), lambda b, h: (b, 0, h)))
        args += [k, v]
    blocks = 2 * (2 * lq * HEAD_LANES + total * (2 * HEAD_LANES + 2 * V_HEAD) + lq * 2 * V_HEAD) * 2
    vmem = blocks + 2 * tq * (total + LANES) * 4 + 2 * total * 2 * V_HEAD * 2 + (16 << 20)
    return pl.pallas_call(
        functools.partial(_attn_kernel, n_kv=len(kvs), tq=tq, tk=tk),
        grid=(bsz, n_pairs),
        in_specs=in_specs,
        out_specs=pl.BlockSpec((None, lq, 2 * V_HEAD), lambda b, h: (b, 0, h)),
        out_shape=jax.ShapeDtypeStruct((bsz, lq, MLA_HEADS * V_HEAD), BF16),
        scratch_shapes=[pltpu.VMEM((2, tq, total), F32), pltpu.VMEM((2, tq, LANES), F32),
                        pltpu.VMEM((2, total, 2 * V_HEAD), BF16)],
        compiler_params=_cparams(("arbitrary", "arbitrary"), vmem),
        name="attn",
    )(*args)


def _post_kernel(*refs, ssd, final):
    it = iter(refs)
    x_ref = next(it)
    if ssd:
        y_ref, z_ref, ng_ref = next(it), next(it), next(it)
    else:
        a_ref = next(it)
    mod_ref, wo_ref, gf_ref, w1_ref, w3_ref, w2_ref = (next(it) for _ in range(6))
    fg_ref = next(it) if final else None
    o_ref = next(it)

    if ssd:
        gated = y_ref[...] * _silu(z_ref[...].astype(F32))
        a = _rms(gated, ng_ref[...]).astype(BF16)
    else:
        a = a_ref[...]
    x1 = x_ref[...] + mod_ref[2:3, :] * _dot(a, wo_ref[...])
    hx = (_rms(x1, gf_ref[...]) * (1.0 + mod_ref[4:5, :]) + mod_ref[3:4, :]).astype(BF16)
    u = _dot(hx, w1_ref[...])
    w = _dot(hx, w3_ref[...])
    t = (_silu(u) * w).astype(BF16)
    x2 = x1 + mod_ref[5:6, :] * _dot(t, w2_ref[...])
    if final:
        x2 = _rms(x2, fg_ref[...])
    o_ref[...] = x2


def _post(x, mix_in, mod, w, tm, final_gain=None):
    bsz, seq, _ = x.shape
    ssd = isinstance(mix_in, tuple)
    row = lambda n: pl.BlockSpec((None, tm, n), lambda b, i: (b, i, 0))
    in_specs = [row(D_MODEL)]
    args = [x]
    if ssd:
        y, z, ng = mix_in
        in_specs += [row(SSD_INNER), row(SSD_INNER), _const_spec((1, SSD_INNER))]
        args += [y, z, ng]
    else:
        in_specs.append(row(mix_in.shape[2]))
        args.append(mix_in)
    in_specs += [_mod_spec(mod), _const_spec(w["wo"].shape), _const_spec((1, D_MODEL)),
                 _const_spec(w["w1"].shape), _const_spec(w["w3"].shape), _const_spec(w["w2"].shape)]
    args += [mod, w["wo"], w["gf"], w["w1"], w["w3"], w["w2"]]
    if final_gain is not None:
        in_specs.append(_const_spec((1, D_MODEL)))
        args.append(final_gain)
    return pl.pallas_call(
        functools.partial(_post_kernel, ssd=ssd, final=final_gain is not None),
        grid=(bsz, seq // tm),
        in_specs=in_specs,
        out_specs=row(D_MODEL),
        out_shape=jax.ShapeDtypeStruct((bsz, seq, D_MODEL), F32),
        compiler_params=_cparams(("arbitrary", "arbitrary"), VMEM_LIMIT_CAP),
        name="post_ssd" if ssd else "post_mla",
    )(*args)


CONV_HALO = 8
CONV_ROWS = 64
CONV_COLS = 512


def _ssd_in_kernel(x_ref, mod_ref, g_ref, wz_ref, wx_ref, wdh_ref, wdl_ref, db_ref, cw_ref, cb_ref,
                   z_ref, u_ref, dt_ref, r_ref):
    i = pl.program_id(1)
    nt = pl.num_programs(1) - 1
    tm = x_ref.shape[0]

    @pl.when(i == 0)
    def _new_sequence():
        r_ref[...] = jnp.zeros_like(r_ref)

    h = _rms(x_ref[...], g_ref[...]) * (1.0 + mod_ref[1:2, :]) + mod_ref[0:1, :]
    hb = h.astype(BF16)
    z_ref[...] = _dot(hb, wz_ref[...]).astype(BF16)
    hl = (h - hb.astype(F32)).astype(BF16)
    raw = _dot(hb, wdh_ref[...]) + _dot(hl, wdh_ref[...]) + _dot(hb, wdl_ref[...])
    raw = raw + db_ref[...]
    dt_ref[...] = jnp.maximum(raw, 0.0) + jnp.log(1.0 + jnp.exp(-jnp.abs(raw)))
    p = _dot(hb, wx_ref[...])
    head = p[0:CONV_HALO, :]
    r_ref[CONV_HALO + tm:2 * CONV_HALO + tm, :] = jnp.where(i == nt, jnp.zeros_like(head), head)

    first = CONV_HALO - SSD_CONV // 2
    for c0 in range(0, r_ref.shape[1], CONV_COLS):
        cs = slice(c0, c0 + CONV_COLS)
        for r0 in range(0, tm, CONV_ROWS):
            acc = jnp.zeros((CONV_ROWS, CONV_COLS), F32) + cb_ref[:, cs]
            for j in range(SSD_CONV):
                lo = first + r0 + j
                acc = acc + r_ref[lo:lo + CONV_ROWS, cs] * cw_ref[j:j + 1, cs]
            u_ref[r0:r0 + CONV_ROWS, cs] = _silu(acc).astype(BF16)

    r_ref[0:CONV_HALO, :] = r_ref[tm:tm + CONV_HALO, :]
    r_ref[CONV_HALO:CONV_HALO + tm, :] = p


def _ssd_in(x, mod, w, tm):
    bsz, seq, _ = x.shape
    nt = seq // tm
    cur = lambda n: pl.BlockSpec((None, tm, n), lambda b, i: (b, jnp.minimum(i, nt - 1), 0))
    prev = lambda n: pl.BlockSpec((None, tm, n), lambda b, i: (b, jnp.maximum(i - 1, 0), 0))
    return pl.pallas_call(
        _ssd_in_kernel,
        grid=(bsz, nt + 1),
        in_specs=[cur(D_MODEL), _mod_spec(mod), _const_spec((1, D_MODEL)),
                  _const_spec(w["wz"].shape), _const_spec(w["wx"].shape),
                  _const_spec(w["wdh"].shape), _const_spec(w["wdl"].shape),
                  _const_spec((1, LANES)), _const_spec((SSD_CONV, SSD_CONV_DIM)),
                  _const_spec((1, SSD_CONV_DIM))],
        out_specs=[cur(SSD_INNER), prev(SSD_CONV_DIM), cur(LANES)],
        out_shape=[jax.ShapeDtypeStruct((bsz, seq, SSD_INNER), BF16),
                   jax.ShapeDtypeStruct((bsz, seq, SSD_CONV_DIM), BF16),
                   jax.ShapeDtypeStruct((bsz, seq, LANES), F32)],
        scratch_shapes=[pltpu.VMEM((tm + 2 * CONV_HALO, SSD_CONV_DIM), F32)],
        compiler_params=_cparams(("arbitrary", "arbitrary"), 52 << 20),
        name="ssd_in",
    )(x, mod, w["g"], w["wz"], w["wx"], w["wdh"], w["wdl"], w["db"], w["cw"], w["cb"])


def _ssd_kernel(xc_ref, bc_ref, cc_ref, dtc_ref, xx_ref, bx_ref, cx_ref, dtx_ref,
                alog_ref, dsk_ref, y_ref, s_ref, w_ref, bt_ref, e_ref, cd_ref):
    g = pl.program_id(1)
    n_ctx = xc_ref.shape[0] // CHUNK
    n_lat = xx_ref.shape[0] // CHUNK
    a_all = jnp.broadcast_to(-jnp.exp(alog_ref[...]), (8, LANES))
    qi = lax.broadcasted_iota(jnp.int32, (CHUNK, CHUNK), 0)
    ki = lax.broadcasted_iota(jnp.int32, (CHUNK, CHUNK), 1)
    lane = lax.broadcasted_iota(jnp.int32, (1, LANES), 1)
    low_half = lane < SSD_HEADDIM
    n_pair = HEADS_PER_GROUP // 2

    def pair_cols(v, pr):
        return jnp.where(low_half, v[:, 2 * pr:2 * pr + 1], v[:, 2 * pr + 1:2 * pr + 2])

    def split3(v):
        hi = v.astype(BF16)
        r1 = v - hi.astype(F32)
        mid = r1.astype(BF16)
        return hi, mid, (r1 - mid.astype(F32)).astype(BF16)

    def lane_bcast(v, col):
        return jnp.broadcast_to(v[:, col:col + 1], (v.shape[0], LANES))

    def prepare(b_ref, c_ref, dt_ref, r0, direction, slot, emit):
        rows = pl.ds(r0, CHUNK)
        causal = (ki <= qi) if direction == 0 else (ki >= qi)
        shift = (LANES - (direction * SSD_HEADS + g * HEADS_PER_GROUP)) & (LANES - 1)
        dts = pltpu.roll(dt_ref[rows, :], shift, 1)
        a_sel = pltpu.roll(a_all, shift, 1)[0:1, :]
        a = dts * a_sel
        tri = jnp.where(causal, 1.0, 0.0).astype(BF16)
        c = sum(_dot(tri, piece) for piece in split3(a))
        c_t = c.T[0:8, :]
        dt_t = dts.T[0:8, :]
        end = CHUNK - 1 if direction == 0 else 0
        to_end_t = jnp.exp(c_t[:, end:end + 1] - c_t) * dt_t
        cd = jnp.exp(c[end:end + 1, :])
        cd_ref[direction, slot] = jnp.broadcast_to(
            jnp.concatenate([pair_cols(cd, pr) for pr in range(n_pair)], axis=1), (8, GROUP_W))
        bmat = b_ref[rows, :]
        b_t = bmat.astype(F32).T
        if emit:
            cb = _dot_nt(c_ref[rows, :], bmat)
        for pr in range(n_pair):
            heads = (2 * pr, 2 * pr + 1)
            wide = slice(pr * 2 * LANES, (pr + 1) * 2 * LANES)
            bt_ref[direction, slot, :, wide] = jnp.concatenate(
                [(b_t * to_end_t[hd:hd + 1, :]).astype(BF16) for hd in heads], axis=1)
            if emit:
                ws, e_b = [], []
                for hd in heads:
                    cq = lane_bcast(c, hd)
                    dec = jnp.exp(jnp.where(causal, cq - c_t[hd:hd + 1, :], -jnp.inf))
                    ws.append((cb * dec * dt_t[hd:hd + 1, :]).astype(BF16))
                    e_b.append(jnp.exp(cq))
                w_ref[direction, slot, :, wide] = jnp.concatenate(ws, axis=1)
                e_ref[direction, slot, :, pr * LANES:(pr + 1) * LANES] = jnp.where(low_half, e_b[0], e_b[1])

    def consume(x_ref, c_ref, r0, direction, slot, emit):
        rows = pl.ds(r0, CHUNK)
        st_ref = s_ref.at[direction]
        x = x_ref[rows, :]
        if emit:
            y_off = _dot(c_ref[rows, :], st_ref[...].astype(BF16))
        ys, upds = [], []
        for pr in range(n_pair):
            ps = slice(pr * LANES, (pr + 1) * LANES)
            wide = slice(pr * 2 * LANES, (pr + 1) * 2 * LANES)
            xp = x[:, ps]
            zero = jnp.zeros_like(xp)
            rhs = jnp.concatenate([jnp.where(low_half, xp, zero),
                                   jnp.where(low_half, zero, xp)], axis=0)
            upds.append(_dot(bt_ref[direction, slot, :, wide], rhs))
            if emit:
                ys.append(_dot(w_ref[direction, slot, :, wide], rhs)
                          + e_ref[direction, slot, :, ps] * y_off[:, ps])
        st_ref[...] = st_ref[...] * cd_ref[direction, slot, 0:1, :] + jnp.concatenate(upds, axis=1)
        return (jnp.concatenate(ys, axis=1), x) if emit else (None, None)

    dsk = dsk_ref[...]
    s_ref[...] = jnp.zeros_like(s_ref)
    for ci in range(n_ctx):
        for direction, cj in ((0, ci), (1, n_ctx - 1 - ci)):
            prepare(bc_ref, cc_ref, dtc_ref, cj * CHUNK, direction, 0, False)
            consume(xc_ref, cc_ref, cj * CHUNK, direction, 0, False)

    def chunk_row(i, direction):
        r0 = (i if direction == 0 else n_lat - 1 - i) * CHUNK
        return r0 if isinstance(r0, int) else pl.multiple_of(r0, CHUNK)

    def stage_prepare(i):
        for direction in range(2):
            prepare(bx_ref, cx_ref, dtx_ref, chunk_row(i, direction), direction, i & 1, True)

    def stage_consume(i, first_touch):
        for direction in range(2):
            r0 = chunk_row(i, direction)
            y, x = consume(xx_ref, cx_ref, r0, direction, i & 1, True)
            if direction == 0:
                y = y + dsk * x.astype(F32)
            if first_touch:
                y_ref[pl.ds(r0, CHUNK), :] = y
            else:
                y_ref[pl.ds(r0, CHUNK), :] = y_ref[pl.ds(r0, CHUNK), :] + y

    def body(i, carry, first_touch):
        stage_prepare(i + 1)
        stage_consume(i, first_touch)
        return carry

    assert n_lat % 2 == 0
    stage_prepare(0)
    lax.fori_loop(0, n_lat // 2, functools.partial(body, first_touch=True), 0)
    lax.fori_loop(n_lat // 2, n_lat - 1, functools.partial(body, first_touch=False), 0)
    stage_consume(n_lat - 1, False)


def _ssd(uc, dtc, ux, dtx, alog, dsk):
    bsz, lc, _ = uc.shape
    lx = ux.shape[1]
    b_blk0 = SSD_INNER // SSD_STATE
    c_blk0 = b_blk0 + SSD_GROUPS

    def seg_specs(rows):
        return [pl.BlockSpec((None, rows, GROUP_W), lambda b, g: (b, 0, g)),
                pl.BlockSpec((None, rows, SSD_STATE), lambda b, g: (b, 0, b_blk0 + g)),
                pl.BlockSpec((None, rows, SSD_STATE), lambda b, g: (b, 0, c_blk0 + g)),
                pl.BlockSpec((None, rows, LANES), lambda b, g: (b, 0, 0))]

    return pl.pallas_call(
        _ssd_kernel,
        grid=(bsz, SSD_GROUPS),
        in_specs=seg_specs(lc) + seg_specs(lx) + [
            pl.BlockSpec((1, LANES), lambda b, g: (0, 0)),
            pl.BlockSpec((1, GROUP_W), lambda b, g: (0, g))],
        out_specs=pl.BlockSpec((None, lx, GROUP_W), lambda b, g: (b, 0, g)),
        out_shape=jax.ShapeDtypeStruct((bsz, lx, SSD_INNER), F32),
        scratch_shapes=[
            pltpu.VMEM((2, SSD_STATE, GROUP_W), F32),
            pltpu.VMEM((2, 2, CHUNK, 2 * GROUP_W), BF16),
            pltpu.VMEM((2, 2, SSD_STATE, 2 * GROUP_W), BF16),
            pltpu.VMEM((2, 2, CHUNK, GROUP_W), F32),
            pltpu.VMEM((2, 2, 8, GROUP_W), F32),
        ],
        compiler_params=_cparams(("arbitrary", "arbitrary"), 48 << 20),
        name="ssd",
    )(uc, uc, uc, dtc, ux, ux, ux, dtx, alog, dsk)


def _rope_partner_index():
    j = np.arange(QK_ROPE)
    return np.where(j % 16 < 8, j + 8, j - 8)


def _rope_tables(seq):
    t = np.arange(seq)
    pos = np.stack([t // GRID_W, t % GRID_W], axis=1).astype(np.float32)
    j = np.arange(QK_ROPE)
    inv = ROPE_THETA ** (-(jnp.arange(8, dtype=F32)) * 2.0 / 16.0)
    ang = jnp.asarray(pos)[:, j // 16] * inv[j % 8][None, :]
    sign = jnp.asarray(np.where(j % 16 < 8, -1.0, 1.0).astype(np.float32))
    return jnp.cos(ang), jnp.sin(ang) * sign


def _mla_tables(seq, rotate):
    scale = math.log2(math.e) / math.sqrt(QK_NOPE + QK_ROPE)
    if rotate:
        cos, sin = _rope_tables(seq)
    else:
        cos, sin = jnp.ones((seq, QK_ROPE), F32), jnp.zeros((seq, QK_ROPE), F32)
    one = jnp.ones((seq, QK_NOPE), F32)
    zero = jnp.zeros((seq, QK_NOPE), F32)
    return {"q": jnp.concatenate([one, cos, sin], axis=1) * scale,
            "kc": jnp.concatenate([zero, cos, cos], axis=1),
            "ks": jnp.concatenate([zero, sin, sin], axis=1)}


def _mla_weights(w_in, q_norm, w_uq, kv_norm, w_ukv):
    partner = _rope_partner_index()
    kr = w_in[:, Q_LORA + KV_LORA:]
    krp = kr[:, partner]
    zero = jnp.zeros((D_MODEL, QK_NOPE), F32)
    win = jnp.concatenate([w_in[:, :Q_LORA + KV_LORA], zero, kr, kr, zero, krp, krp], axis=1)
    uq = w_uq.reshape(Q_LORA, MLA_HEADS, QK_NOPE + QK_ROPE)
    rope = uq[:, :, QK_NOPE:]
    wq = jnp.concatenate([uq, rope[:, :, partner]], axis=2).reshape(Q_LORA, MLA_HEADS * HEAD_LANES)
    ukv = w_ukv.reshape(KV_LORA, MLA_HEADS, QK_NOPE + V_HEAD)
    wk = jnp.concatenate([ukv[:, :, :QK_NOPE], jnp.zeros((KV_LORA, MLA_HEADS, QK_NOPE), F32)], axis=2)
    return {"win": win.astype(BF16), "qg": q_norm.reshape(1, -1), "wq": wq.astype(BF16),
            "kvg": kv_norm.reshape(1, -1), "wk": wk.reshape(KV_LORA, -1).astype(BF16),
            "wv": ukv[:, :, QK_NOPE:].reshape(KV_LORA, -1).astype(BF16)}


def _post_weights(wo, gf, w1, w3, w2):
    return {"wo": wo.astype(BF16), "gf": gf.reshape(1, -1), "w1": w1.astype(BF16),
            "w3": w3.astype(BF16), "w2": w2.astype(BF16)}


def _ssd_in_weights(gain, w_in, dt_bias, conv_w, conv_b):
    wd = jnp.pad(w_in[:, SSD_INNER + SSD_CONV_DIM:], ((0, 0), (0, LANES - 2 * SSD_HEADS)))
    wdh = wd.astype(BF16)
    wdl = (wd - wdh.astype(F32)).astype(BF16)
    db = jnp.pad(dt_bias.reshape(1, -1), ((0, 0), (0, LANES - 2 * SSD_HEADS)))
    return {"g": gain.reshape(1, -1), "wz": w_in[:, :SSD_INNER].astype(BF16),
            "wx": w_in[:, SSD_INNER:SSD_INNER + SSD_CONV_DIM].astype(BF16),
            "wdh": wdh, "wdl": wdl, "db": db, "cw": conv_w, "cb": conv_b.reshape(1, -1)}


def kernel(x, c, ctx, c_ctx, ada_w, ada_b, norm_mix, norm_ffn, ffn_w1, ffn_w3, ffn_w2,
           mla_w_in, mla_q_norm, mla_w_uq, mla_kv_norm, mla_w_ukv, mla_w_o,
           ssd_w_in, ssd_conv_w, ssd_conv_b, ssd_dt_bias, ssd_a_log, ssd_d, ssd_norm, ssd_w_o,
           final_norm):
    bsz, seq, _ = x.shape
    lc = ctx.shape[1]
    tm = 256

    cc = jnp.zeros((8, D_MODEL), F32).at[:bsz].set(c).at[bsz].set(c_ctx)
    mod = _ada(cc, ada_w, ada_b)
    mod_x = [mod[l, :bsz].reshape(bsz, 6, D_MODEL) for l in range(DEPTH)]
    mod_c = [mod[l, bsz:bsz + 1].reshape(1, 6, D_MODEL) for l in range(DEPTH)]

    wm = _mla_weights(mla_w_in[0], mla_q_norm[0], mla_w_uq[0], mla_kv_norm[0], mla_w_ukv[0])
    g0 = norm_mix[0].reshape(1, -1)
    q_c, k_c, v_c = _mla_proj(ctx, mod_c[0], g0, wm, _mla_tables(lc, False), tm)
    q_x, k_x, v_x = _mla_proj(x, mod_x[0], g0, wm, _mla_tables(seq, True), tm)
    o_x = _attn(q_x, [(k_c, v_c), (k_x, v_x)], tq=256, tk=512)
    o_c = _attn(q_c, [(k_c, v_c)], tq=lc, tk=512)
    wp0 = _post_weights(mla_w_o[0], norm_ffn[0], ffn_w1[0], ffn_w3[0], ffn_w2[0])
    x = _post(x, o_x, mod_x[0], wp0, tm)
    ctx = _post(ctx, o_c, mod_c[0], wp0, tm)

    ws = _ssd_in_weights(norm_mix[1], ssd_w_in[0], ssd_dt_bias[0], ssd_conv_w[0], ssd_conv_b[0])
    _, u_c, dt_c = _ssd_in(ctx, mod_c[1], ws, tm)
    z_x, u_x, dt_x = _ssd_in(x, mod_x[1], ws, tm)
    alog = jnp.pad(ssd_a_log[0].reshape(1, -1), ((0, 0), (0, LANES - 2 * SSD_HEADS)))
    dsk = jnp.repeat(ssd_d[0], SSD_HEADDIM).reshape(1, -1)
    y = _ssd(u_c, dt_c, u_x, dt_x, alog, dsk)
    wp1 = _post_weights(ssd_w_o[0], norm_ffn[1], ffn_w1[1], ffn_w3[1], ffn_w2[1])
    return _post(x, (y, z_x, ssd_norm[0].reshape(1, -1)), mod_x[1], wp1, tm,
                 final_gain=final_norm.reshape(1, -1))
```

```python
import functools
import math

import numpy as np
import jax
import jax.numpy as jnp
from jax import lax
from jax.experimental import pallas as pl
from jax.experimental.pallas import tpu as pltpu

D_MODEL = 1024
DEPTH = 2
GRID_W = 64
EPS = 1e-6
MLA_HEADS = 16
Q_LORA = 512
KV_LORA = 256
QK_NOPE = 64
QK_ROPE = 32
V_HEAD = 64
ROPE_THETA = 10000.0
SSD_INNER = 2 * D_MODEL
SSD_HEADDIM = 64
SSD_HEADS = SSD_INNER // SSD_HEADDIM
SSD_GROUPS = 4
SSD_STATE = 128
SSD_CONV = 5
SSD_CONV_DIM = SSD_INNER + 2 * SSD_GROUPS * SSD_STATE
CHUNK = 128
FFN_HIDDEN = ((8 * D_MODEL // 3 + 255) // 256) * 256

LANES = 128
V7X_VMEM_BYTES = 64 * 1024 * 1024
VMEM_LIMIT_CAP = 56 * 1024 * 1024

F32 = jnp.float32
BF16 = jnp.bfloat16
HEAD_LANES = 2 * QK_NOPE
HEADS_PER_GROUP = SSD_HEADS // SSD_GROUPS
GROUP_W = HEADS_PER_GROUP * SSD_HEADDIM


def _cparams(semantics, vmem_bytes):
    return pltpu.CompilerParams(
        dimension_semantics=semantics,
        vmem_limit_bytes=int(min(VMEM_LIMIT_CAP, vmem_bytes)),
    )


def _const_spec(shape):
    nd = len(shape)
    return pl.BlockSpec(shape, lambda *_: (0,) * nd, pipeline_mode=pl.Buffered(1))


def _rms(x, g):
    return x * lax.rsqrt(jnp.mean(x * x, axis=-1, keepdims=True) + EPS) * g


def _silu(x):
    return x * (1.0 / (1.0 + jnp.exp(-x)))


def _dot(a, b):
    return jnp.dot(a, b, preferred_element_type=F32)


def _dot_nt(a, b):
    return lax.dot_general(a, b, (((1,), (1,)), ((), ())), preferred_element_type=F32)


def _dot_f32(a, b):
    return jnp.dot(a, b, preferred_element_type=F32, precision=lax.Precision.HIGHEST)


def _ada_kernel(c_ref, w_ref, b_ref, o_ref):
    s = _silu(c_ref[...])
    o_ref[...] = _dot_f32(s, w_ref[...]) + b_ref[...]


def _ada(cc, ada_w, ada_b):
    n_tiles = ada_w.shape[2] // D_MODEL
    return pl.pallas_call(
        _ada_kernel,
        grid=(DEPTH, n_tiles),
        in_specs=[
            pl.BlockSpec((8, D_MODEL), lambda l, j: (0, 0)),
            pl.BlockSpec((None, D_MODEL, D_MODEL), lambda l, j: (l, 0, j)),
            pl.BlockSpec((None, 1, D_MODEL), lambda l, j: (l, 0, j)),
        ],
        out_specs=pl.BlockSpec((None, 8, D_MODEL), lambda l, j: (l, 0, j)),
        out_shape=jax.ShapeDtypeStruct((DEPTH, 8, ada_w.shape[2]), F32),
        compiler_params=_cparams(("arbitrary", "arbitrary"), 24 << 20),
        name="ada",
    )(cc, ada_w, ada_b.reshape(DEPTH, 1, -1))


def _mod_spec(mod):
    if mod.shape[0] == 1:
        return pl.BlockSpec((None, 6, D_MODEL), lambda b, i: (0, 0, 0))
    return pl.BlockSpec((None, 6, D_MODEL), lambda b, i: (b, 0, 0))


def _mla_proj_kernel(x_ref, mod_ref, g_ref, win_ref, qg_ref, wq_ref, kvg_ref, wk_ref, wv_ref,
                     tq_ref, tkc_ref, tks_ref, q_ref, k_ref, v_ref):
    x = x_ref[...]
    h = _rms(x, g_ref[...]) * (1.0 + mod_ref[1:2, :]) + mod_ref[0:1, :]
    p = _dot(h.astype(BF16), win_ref[...])
    cq = p[:, :Q_LORA]
    ckv = p[:, Q_LORA:Q_LORA + KV_LORA]
    kr_a = p[:, Q_LORA + KV_LORA:Q_LORA + KV_LORA + LANES]
    kr_b = p[:, Q_LORA + KV_LORA + LANES:]
    cqn = _rms(cq, qg_ref[...]).astype(BF16)
    ckvn = _rms(ckv, kvg_ref[...]).astype(BF16)
    qf = _dot(cqn, wq_ref[...])
    kf = _dot(ckvn, wk_ref[...])
    kr = kr_a * tkc_ref[...] + kr_b * tks_ref[...]
    tq = tq_ref[...]
    for hd in range(MLA_HEADS):
        sl = slice(hd * HEAD_LANES, (hd + 1) * HEAD_LANES)
        q_ref[hd] = (qf[:, sl] * tq).astype(BF16)
        k_ref[hd] = (kf[:, sl] + kr).astype(BF16)
    v_ref[...] = _dot(ckvn, wv_ref[...]).astype(BF16)


def _mla_proj(x, mod, gain, w, tabs, tm):
    bsz, seq, _ = x.shape
    row = lambda n: pl.BlockSpec((None, tm, n), lambda b, i: (b, i, 0))
    heads = pl.BlockSpec((None, MLA_HEADS, tm, HEAD_LANES), lambda b, i: (b, 0, i, 0))
    tab = pl.BlockSpec((tm, LANES), lambda b, i: (i, 0))
    return pl.pallas_call(
        _mla_proj_kernel,
        grid=(bsz, seq // tm),
        in_specs=[
            row(D_MODEL), _mod_spec(mod), _const_spec((1, D_MODEL)),
            _const_spec(w["win"].shape), _const_spec((1, Q_LORA)), _const_spec(w["wq"].shape),
            _const_spec((1, KV_LORA)), _const_spec(w["wk"].shape), _const_spec(w["wv"].shape),
            tab, tab, tab,
        ],
        out_specs=[heads, heads, row(MLA_HEADS * V_HEAD)],
        out_shape=[
            jax.ShapeDtypeStruct((bsz, MLA_HEADS, seq, HEAD_LANES), BF16),
            jax.ShapeDtypeStruct((bsz, MLA_HEADS, seq, HEAD_LANES), BF16),
            jax.ShapeDtypeStruct((bsz, seq, MLA_HEADS * V_HEAD), BF16),
        ],
        compiler_params=_cparams(("arbitrary", "arbitrary"), 40 << 20),
        name="mla_proj",
    )(x, mod, gain, w["win"], w["qg"], w["wq"], w["kvg"], w["wk"], w["wv"],
      tabs["q"], tabs["kc"], tabs["ks"])


def _attn_kernel(*refs, n_kv, tq, tk):
    q_ref = refs[0]
    kv_refs = [(refs[1 + 2 * j], refs[2 + 2 * j]) for j in range(n_kv)]
    o_ref = refs[1 + 2 * n_kv]
    s_ref, m_ref, vm_ref = refs[2 + 2 * n_kv:]
    n_q = q_ref.shape[1] // tq
    lane = lax.broadcasted_iota(jnp.int32, (1, LANES), 1)
    low_half = lane < V_HEAD
    chunks = []
    col = 0
    for k_ref, v_ref in kv_refs:
        rows = k_ref.shape[1]
        v = v_ref[...]
        one = jnp.ones_like(v)
        vm_ref[0, col:col + rows, :] = jnp.where(low_half, v, one)
        vm_ref[1, col:col + rows, :] = jnp.where(low_half, one, v)
        step = min(tk, rows)
        for r0 in range(0, rows, step):
            chunks.append((k_ref, r0, step, col + r0))
        col += rows

    def tile_rows(i):
        start = i * tq
        return pl.ds(start if isinstance(start, int) else pl.multiple_of(start, tq), tq)

    def scores(i, hh):
        qh = q_ref[hh, tile_rows(i), :]
        m_lanes = None
        for k_ref, r0, rows, c0 in chunks:
            s = _dot_nt(qh, k_ref[hh, r0:r0 + rows, :])
            s_ref[hh, :, c0:c0 + rows] = s
            blocks = [s[:, j:j + LANES] for j in range(0, rows, LANES)]
            cm = functools.reduce(jnp.maximum, blocks)
            m_lanes = cm if m_lanes is None else jnp.maximum(m_lanes, cm)
        m = jnp.max(m_lanes, axis=-1, keepdims=True)
        m_ref[hh] = jnp.broadcast_to(m, (tq, LANES))

    def softmax_pv(hh):
        acc = jnp.zeros((tq, LANES), F32)
        for _, _, rows, c0 in chunks:
            ps = [jnp.exp2(s_ref[hh, :, c0 + j:c0 + j + LANES] - m_ref[hh]).astype(BF16)
                  for j in range(0, rows, LANES)]
            acc = acc + _dot(jnp.concatenate(ps, axis=1), vm_ref[hh, c0:c0 + rows, :])
        return acc

    def tile(i, next_scores):
        scores(i, 1)
        acc0 = softmax_pv(0)
        if next_scores:
            scores(i + 1, 0)
        acc1 = softmax_pv(1)
        out = jnp.where(low_half, acc0 / pltpu.roll(acc0, V_HEAD, 1), acc1 / pltpu.roll(acc1, V_HEAD, 1))
        o_ref[tile_rows(i), :] = out.astype(BF16)

    scores(0, 0)

    def body(i, carry):
        tile(i, True)
        return carry

    lax.fori_loop(0, n_q - 1, body, 0)
    tile(n_q - 1, False)


def _attn(q, kvs, tq, tk):
    bsz, _, lq, _ = q.shape
    n_pairs = MLA_HEADS // 2
    total = sum(k.shape[2] for k, _ in kvs)
    in_specs = [pl.BlockSpec((None, 2, lq, HEAD_LANES), lambda b, h: (b, h, 0, 0))]
    args = [q]
    for k, v in kvs:
        t = k.shape[2]
        in_specs.append(pl.BlockSpec((None, 2, t, HEAD_LANES), lambda b, h: (b, h, 0, 0)))
        in_specs.append(pl.BlockSpec((None, t, 2 * V_HEAD), lambda b, h: (b, 0, h)))
        args += [k, v]
    blocks = 2 * (2 * lq * HEAD_LANES + total * (2 * HEAD_LANES + 2 * V_HEAD) + lq * 2 * V_HEAD) * 2
    vmem = blocks + 2 * tq * (total + LANES) * 4 + 2 * total * 2 * V_HEAD * 2 + (16 << 20)
    return pl.pallas_call(
        functools.partial(_attn_kernel, n_kv=len(kvs), tq=tq, tk=tk),
        grid=(bsz, n_pairs),
        in_specs=in_specs,
        out_specs=pl.BlockSpec((None, lq, 2 * V_HEAD), lambda b, h: (b, 0, h)),
        out_shape=jax.ShapeDtypeStruct((bsz, lq, MLA_HEADS * V_HEAD), BF16),
        scratch_shapes=[pltpu.VMEM((2, tq, total), F32), pltpu.VMEM((2, tq, LANES), F32),
                        pltpu.VMEM((2, total, 2 * V_HEAD), BF16)],
        compiler_params=_cparams(("arbitrary", "arbitrary"), vmem),
        name="attn",
    )(*args)


def _post_kernel(*refs, ssd, final):
    it = iter(refs)
    x_ref = next(it)
    if ssd:
        y_ref, z_ref, ng_ref = next(it), next(it), next(it)
    else:
        a_ref = next(it)
    mod_ref, wo_ref, gf_ref, w1_ref, w3_ref, w2_ref = (next(it) for _ in range(6))
    fg_ref = next(it) if final else None
    o_ref = next(it)

    if ssd:
        gated = y_ref[...].astype(F32) * _silu(z_ref[...].astype(F32))
        a = _rms(gated, ng_ref[...]).astype(BF16)
    else:
        a = a_ref[...]
    x1 = x_ref[...] + mod_ref[2:3, :] * _dot(a, wo_ref[...])
    hx = (_rms(x1, gf_ref[...]) * (1.0 + mod_ref[4:5, :]) + mod_ref[3:4, :]).astype(BF16)
    u = _dot(hx, w1_ref[...])
    w = _dot(hx, w3_ref[...])
    t = (_silu(u) * w).astype(BF16)
    x2 = x1 + mod_ref[5:6, :] * _dot(t, w2_ref[...])
    if final:
        x2 = _rms(x2, fg_ref[...])
    o_ref[...] = x2


def _post(x, mix_in, mod, w, tm, final_gain=None):
    bsz, seq, _ = x.shape
    ssd = isinstance(mix_in, tuple)
    row = lambda n: pl.BlockSpec((None, tm, n), lambda b, i: (b, i, 0))
    in_specs = [row(D_MODEL)]
    args = [x]
    if ssd:
        y, z, ng = mix_in
        in_specs += [row(SSD_INNER), row(SSD_INNER), _const_spec((1, SSD_INNER))]
        args += [y, z, ng]
    else:
        in_specs.append(row(mix_in.shape[2]))
        args.append(mix_in)
    in_specs += [_mod_spec(mod), _const_spec(w["wo"].shape), _const_spec((1, D_MODEL)),
                 _const_spec(w["w1"].shape), _const_spec(w["w3"].shape), _const_spec(w["w2"].shape)]
    args += [mod, w["wo"], w["gf"], w["w1"], w["w3"], w["w2"]]
    if final_gain is not None:
        in_specs.append(_const_spec((1, D_MODEL)))
        args.append(final_gain)
    return pl.pallas_call(
        functools.partial(_post_kernel, ssd=ssd, final=final_gain is not None),
        grid=(bsz, seq // tm),
        in_specs=in_specs,
        out_specs=row(D_MODEL),
        out_shape=jax.ShapeDtypeStruct((bsz, seq, D_MODEL), F32),
        compiler_params=_cparams(("arbitrary", "arbitrary"), VMEM_LIMIT_CAP),
        name="post_ssd" if ssd else "post_mla",
    )(*args)


def _ssd_in_kernel(x_ref, mod_ref, g_ref, wz_ref, wx_ref, wdh_ref, wdl_ref, db_ref,
                   z_ref, xbc_ref, dt_ref):
    h = _rms(x_ref[...], g_ref[...]) * (1.0 + mod_ref[1:2, :]) + mod_ref[0:1, :]
    hb = h.astype(BF16)
    z_ref[...] = _dot(hb, wz_ref[...]).astype(BF16)
    xbc_ref[...] = _dot(hb, wx_ref[...]).astype(BF16)
    hl = (h - hb.astype(F32)).astype(BF16)
    raw = _dot(hb, wdh_ref[...]) + _dot(hl, wdh_ref[...]) + _dot(hb, wdl_ref[...])
    raw = raw + db_ref[...]
    dt_ref[...] = jnp.maximum(raw, 0.0) + jnp.log(1.0 + jnp.exp(-jnp.abs(raw)))


def _ssd_in(x, mod, w, tm):
    bsz, seq, _ = x.shape
    row = lambda n: pl.BlockSpec((None, tm, n), lambda b, i: (b, i, 0))
    return pl.pallas_call(
        _ssd_in_kernel,
        grid=(bsz, seq // tm),
        in_specs=[row(D_MODEL), _mod_spec(mod), _const_spec((1, D_MODEL)),
                  _const_spec(w["wz"].shape), _const_spec(w["wx"].shape),
                  _const_spec(w["wdh"].shape), _const_spec(w["wdl"].shape),
                  _const_spec((1, LANES))],
        out_specs=[row(SSD_INNER), row(SSD_CONV_DIM), row(LANES)],
        out_shape=[jax.ShapeDtypeStruct((bsz, seq, SSD_INNER), BF16),
                   jax.ShapeDtypeStruct((bsz, seq, SSD_CONV_DIM), BF16),
                   jax.ShapeDtypeStruct((bsz, seq, LANES), F32)],
        compiler_params=_cparams(("arbitrary", "arbitrary"), 48 << 20),
        name="ssd_in",
    )(x, mod, w["g"], w["wz"], w["wx"], w["wdh"], w["wdl"], w["db"])


CONV_PAD = 16
LOG2E = math.log2(math.e)


def _ssd_kernel(xc_ref, bc_ref, cc_ref, dtc_ref, xx_ref, bx_ref, cx_ref, dtx_ref,
                alog_ref, dsk_ref, cwx_ref, cwb_ref, cwc_ref, cbx_ref, cbb_ref, cbc_ref,
                y_ref, s_ref, w_ref, bt_ref, e_ref, cd_ref,
                cum_ref, cumt_ref, dtt_ref, btr_ref, ux_ref, ub_ref, uc_ref, shift_ref):
    g = pl.program_id(1)
    n_ctx = xc_ref.shape[0] // CHUNK
    n_lat = xx_ref.shape[0] // CHUNK
    a_all = jnp.broadcast_to(-jnp.exp(alog_ref[...]), (8, LANES))
    qi = lax.broadcasted_iota(jnp.int32, (CHUNK, CHUNK), 0)
    ki = lax.broadcasted_iota(jnp.int32, (CHUNK, CHUNK), 1)
    lane = lax.broadcasted_iota(jnp.int32, (1, LANES), 1)
    low_half = lane < SSD_HEADDIM
    n_pair = HEADS_PER_GROUP // 2

    def pair_cols(v, pr):
        return jnp.where(low_half, v[:, 2 * pr:2 * pr + 1], v[:, 2 * pr + 1:2 * pr + 2])

    def split3(v):
        hi = v.astype(BF16)
        r1 = v - hi.astype(F32)
        mid = r1.astype(BF16)
        return hi, mid, (r1 - mid.astype(F32)).astype(BF16)

    def lane_bcast(v, col):
        return jnp.broadcast_to(v[:, col:col + 1], (v.shape[0], LANES))

    sel_r = lax.broadcasted_iota(jnp.int32, (CHUNK, CHUNK + 2 * CONV_PAD), 0)
    sel_c = lax.broadcasted_iota(jnp.int32, (CHUNK, CHUNK + 2 * CONV_PAD), 1)
    for j in range(SSD_CONV):
        hit = sel_c == sel_r + (CONV_PAD - SSD_CONV // 2 + j)
        shift_ref[j * CHUNK:(j + 1) * CHUNK, :] = jnp.where(hit, 1.0, 0.0).astype(BF16)

    def window(raw_ref, r0, first, last):
        pad = jnp.zeros((CONV_PAD, raw_ref.shape[1]), BF16)
        if first and last:
            return jnp.concatenate([pad, raw_ref[0:CHUNK, :], pad], axis=0)
        if first:
            return jnp.concatenate([pad, raw_ref[0:CHUNK + CONV_PAD, :]], axis=0)
        if last:
            return jnp.concatenate([raw_ref[r0 - CONV_PAD:r0 + CHUNK, :], pad], axis=0)
        return raw_ref[pl.ds(pl.multiple_of(r0 - CONV_PAD, CONV_PAD), CHUNK + 2 * CONV_PAD), :]

    def scan_sums(x_ref, b_ref, c_ref, dt_ref, r0, chunk_id, first, last):
        grow = pl.ds(chunk_id * CHUNK if isinstance(chunk_id, int)
                     else pl.multiple_of(chunk_id * CHUNK, CHUNK), CHUNK)
        win = jnp.concatenate([window(r, r0, first, last) for r in (x_ref, b_ref, c_ref)], axis=1)
        taps = _dot(shift_ref[...], win)
        cw = jnp.concatenate([cwx_ref[...], cwb_ref[...], cwc_ref[...]], axis=1)
        acc = jnp.concatenate([cbx_ref[...], cbb_ref[...], cbc_ref[...]], axis=1)
        for j in range(SSD_CONV):
            acc = acc + taps[j * CHUNK:(j + 1) * CHUNK, :] * cw[j:j + 1, :]
        u = _silu(acc).astype(BF16)
        ub = u[:, GROUP_W:GROUP_W + SSD_STATE]
        ux_ref[grow, :] = u[:, :GROUP_W]
        ub_ref[grow, :] = ub
        uc_ref[grow, :] = u[:, GROUP_W + SSD_STATE:]
        btr_ref[chunk_id] = ub.astype(F32).T
        dt = dt_ref[pl.ds(r0, CHUNK), :]
        for direction in range(2):
            causal = (ki <= qi) if direction == 0 else (ki >= qi)
            shift = (LANES - (direction * SSD_HEADS + g * HEADS_PER_GROUP)) & (LANES - 1)
            dts = pltpu.roll(dt, shift, 1)
            a = dts * pltpu.roll(a_all, shift, 1)[0:1, :]
            tri = jnp.where(causal, 1.0, 0.0).astype(BF16)
            c = sum(_dot(tri, piece) for piece in split3(a)) * LOG2E
            cum_ref[direction, chunk_id] = c
            cumt_ref[direction, chunk_id] = c.T[0:8, :]
            dtt_ref[direction, chunk_id] = dts.T[0:8, :]

    def prepare(chunk_id, direction, slot, emit):
        grow = pl.ds(chunk_id * CHUNK if isinstance(chunk_id, int)
                     else pl.multiple_of(chunk_id * CHUNK, CHUNK), CHUNK)
        causal = (ki <= qi) if direction == 0 else (ki >= qi)
        c = cum_ref[direction, chunk_id]
        c_t = cumt_ref[direction, chunk_id]
        dt_t = dtt_ref[direction, chunk_id]
        end = CHUNK - 1 if direction == 0 else 0
        to_end_t = jnp.exp2(c_t[:, end:end + 1] - c_t) * dt_t
        cd = jnp.exp2(c[end:end + 1, :])
        cd_ref[direction, slot] = jnp.broadcast_to(
            jnp.concatenate([pair_cols(cd, pr) for pr in range(n_pair)], axis=1), (8, GROUP_W))
        b_t = btr_ref[chunk_id]
        if emit:
            cb = _dot_nt(uc_ref[grow, :], ub_ref[grow, :])
        for pr in range(n_pair):
            heads = (2 * pr, 2 * pr + 1)
            wide = slice(pr * 2 * LANES, (pr + 1) * 2 * LANES)
            bt_ref[direction, slot, :, wide] = jnp.concatenate(
                [(b_t * to_end_t[hd:hd + 1, :]).astype(BF16) for hd in heads], axis=1)
            if emit:
                ws, e_b = [], []
                for hd in heads:
                    cq = lane_bcast(c, hd)
                    dec = jnp.exp2(jnp.where(causal, cq - c_t[hd:hd + 1, :], -jnp.inf))
                    ws.append((cb * dec * dt_t[hd:hd + 1, :]).astype(BF16))
                    e_b.append(jnp.exp2(cq))
                w_ref[direction, slot, :, wide] = jnp.concatenate(ws, axis=1)
                e_ref[direction, slot, :, pr * LANES:(pr + 1) * LANES] = jnp.where(low_half, e_b[0], e_b[1])

    def consume(chunk_id, direction, slot, emit):
        grow = pl.ds(chunk_id * CHUNK if isinstance(chunk_id, int)
                     else pl.multiple_of(chunk_id * CHUNK, CHUNK), CHUNK)
        st_ref = s_ref.at[direction]
        x = ux_ref[grow, :]
        if emit:
            y_off = _dot(uc_ref[grow, :], st_ref[...].astype(BF16))
        ys, upds = [], []
        for pr in range(n_pair):
            ps = slice(pr * LANES, (pr + 1) * LANES)
            wide = slice(pr * 2 * LANES, (pr + 1) * 2 * LANES)
            xp = x[:, ps]
            zero = jnp.zeros_like(xp)
            rhs = jnp.concatenate([jnp.where(low_half, xp, zero),
                                   jnp.where(low_half, zero, xp)], axis=0)
            upds.append(_dot(bt_ref[direction, slot, :, wide], rhs))
            if emit:
                ys.append(_dot(w_ref[direction, slot, :, wide], rhs)
                          + e_ref[direction, slot, :, ps] * y_off[:, ps])
        st_ref[...] = st_ref[...] * cd_ref[direction, slot, 0:1, :] + jnp.concatenate(upds, axis=1)
        return (jnp.concatenate(ys, axis=1), x) if emit else (None, None)

    dsk = dsk_ref[...]
    s_ref[...] = jnp.zeros_like(s_ref)
    for ci in range(n_ctx):
        scan_sums(xc_ref, bc_ref, cc_ref, dtc_ref, ci * CHUNK, ci, ci == 0, ci == n_ctx - 1)

    assert n_lat >= 2
    scan_sums(xx_ref, bx_ref, cx_ref, dtx_ref, 0, n_ctx, True, False)

    def sums_body(j, carry):
        scan_sums(xx_ref, bx_ref, cx_ref, dtx_ref, pl.multiple_of(j * CHUNK, CHUNK), n_ctx + j,
                  False, False)
        return carry

    lax.fori_loop(1, n_lat - 1, sums_body, 0, unroll=2)
    scan_sums(xx_ref, bx_ref, cx_ref, dtx_ref, (n_lat - 1) * CHUNK, n_ctx + n_lat - 1, False, True)

    for ci in range(n_ctx):
        for direction, cj in ((0, ci), (1, n_ctx - 1 - ci)):
            prepare(cj, direction, 0, False)
            consume(cj, direction, 0, False)

    def chunk_of(i, direction):
        return i if direction == 0 else n_lat - 1 - i

    def chunk_row(i, direction):
        r0 = chunk_of(i, direction) * CHUNK
        return r0 if isinstance(r0, int) else pl.multiple_of(r0, CHUNK)

    def stage_prepare(i):
        for direction in range(2):
            prepare(n_ctx + chunk_of(i, direction), direction, i & 1, True)

    def stage_consume(i, first_touch):
        for direction in range(2):
            rows = pl.ds(chunk_row(i, direction), CHUNK)
            y, x = consume(n_ctx + chunk_of(i, direction), direction, i & 1, True)
            if direction == 0:
                y = y + dsk * x.astype(F32)
            if not first_touch:
                y = y + y_ref[rows, :].astype(F32)
            y_ref[rows, :] = y.astype(BF16)

    def body(i, carry, first_touch):
        stage_prepare(i + 1)
        stage_consume(i, first_touch)
        return carry

    assert n_lat % 2 == 0
    stage_prepare(0)
    lax.fori_loop(0, n_lat // 2, functools.partial(body, first_touch=True), 0)
    lax.fori_loop(n_lat // 2, n_lat - 1, functools.partial(body, first_touch=False), 0)
    stage_consume(n_lat - 1, False)


def _ssd(uc, dtc, ux, dtx, alog, dsk, conv_w, conv_b):
    bsz, lc, _ = uc.shape
    lx = ux.shape[1]
    n_chunks = (lc + lx) // CHUNK
    b_blk0 = SSD_INNER // SSD_STATE
    c_blk0 = b_blk0 + SSD_GROUPS
    chan_specs = [(GROUP_W, lambda b, g: (0, g)),
                  (SSD_STATE, lambda b, g: (0, b_blk0 + g)),
                  (SSD_STATE, lambda b, g: (0, c_blk0 + g))]

    def seg_specs(rows):
        return [pl.BlockSpec((None, rows, GROUP_W), lambda b, g: (b, 0, g)),
                pl.BlockSpec((None, rows, SSD_STATE), lambda b, g: (b, 0, b_blk0 + g)),
                pl.BlockSpec((None, rows, SSD_STATE), lambda b, g: (b, 0, c_blk0 + g)),
                pl.BlockSpec((None, rows, LANES), lambda b, g: (b, 0, 0))]

    return pl.pallas_call(
        _ssd_kernel,
        grid=(bsz, SSD_GROUPS),
        in_specs=seg_specs(lc) + seg_specs(lx) + [
            pl.BlockSpec((1, LANES), lambda b, g: (0, 0)),
            pl.BlockSpec((1, GROUP_W), lambda b, g: (0, g))]
        + [pl.BlockSpec((SSD_CONV, wd), f) for wd, f in chan_specs]
        + [pl.BlockSpec((1, wd), f) for wd, f in chan_specs],
        out_specs=pl.BlockSpec((None, lx, GROUP_W), lambda b, g: (b, 0, g)),
        out_shape=jax.ShapeDtypeStruct((bsz, lx, SSD_INNER), BF16),
        scratch_shapes=[
            pltpu.VMEM((2, SSD_STATE, GROUP_W), F32),
            pltpu.VMEM((2, 2, CHUNK, 2 * GROUP_W), BF16),
            pltpu.VMEM((2, 2, SSD_STATE, 2 * GROUP_W), BF16),
            pltpu.VMEM((2, 2, CHUNK, GROUP_W), F32),
            pltpu.VMEM((2, 2, 8, GROUP_W), F32),
            pltpu.VMEM((2, n_chunks, CHUNK, LANES), F32),
            pltpu.VMEM((2, n_chunks, 8, CHUNK), F32),
            pltpu.VMEM((2, n_chunks, 8, CHUNK), F32),
            pltpu.VMEM((n_chunks, SSD_STATE, CHUNK), F32),
            pltpu.VMEM((lc + lx, GROUP_W), BF16),
            pltpu.VMEM((lc + lx, SSD_STATE), BF16),
            pltpu.VMEM((lc + lx, SSD_STATE), BF16),
            pltpu.VMEM((SSD_CONV * CHUNK, CHUNK + 2 * CONV_PAD), BF16),
        ],
        compiler_params=_cparams(("arbitrary", "arbitrary"), 54 << 20),
        name="ssd",
    )(uc, uc, uc, dtc, ux, ux, ux, dtx, alog, dsk, conv_w, conv_w, conv_w, conv_b, conv_b, conv_b)


def _rope_partner_index():
    j = np.arange(QK_ROPE)
    return np.where(j % 16 < 8, j + 8, j - 8)


def _rope_tables(seq):
    t = np.arange(seq)
    pos = np.stack([t // GRID_W, t % GRID_W], axis=1).astype(np.float32)
    j = np.arange(QK_ROPE)
    inv = ROPE_THETA ** (-(jnp.arange(8, dtype=F32)) * 2.0 / 16.0)
    ang = jnp.asarray(pos)[:, j // 16] * inv[j % 8][None, :]
    sign = jnp.asarray(np.where(j % 16 < 8, -1.0, 1.0).astype(np.float32))
    return jnp.cos(ang), jnp.sin(ang) * sign


def _mla_tables(seq, rotate):
    scale = math.log2(math.e) / math.sqrt(QK_NOPE + QK_ROPE)
    if rotate:
        cos, sin = _rope_tables(seq)
    else:
        cos, sin = jnp.ones((seq, QK_ROPE), F32), jnp.zeros((seq, QK_ROPE), F32)
    one = jnp.ones((seq, QK_NOPE), F32)
    zero = jnp.zeros((seq, QK_NOPE), F32)
    return {"q": jnp.concatenate([one, cos, sin], axis=1) * scale,
            "kc": jnp.concatenate([zero, cos, cos], axis=1),
            "ks": jnp.concatenate([zero, sin, sin], axis=1)}


def _mla_weights(w_in, q_norm, w_uq, kv_norm, w_ukv):
    partner = _rope_partner_index()
    kr = w_in[:, Q_LORA + KV_LORA:]
    krp = kr[:, partner]
    zero = jnp.zeros((D_MODEL, QK_NOPE), F32)
    win = jnp.concatenate([w_in[:, :Q_LORA + KV_LORA], zero, kr, kr, zero, krp, krp], axis=1)
    uq = w_uq.reshape(Q_LORA, MLA_HEADS, QK_NOPE + QK_ROPE)
    rope = uq[:, :, QK_NOPE:]
    wq = jnp.concatenate([uq, rope[:, :, partner]], axis=2).reshape(Q_LORA, MLA_HEADS * HEAD_LANES)
    ukv = w_ukv.reshape(KV_LORA, MLA_HEADS, QK_NOPE + V_HEAD)
    wk = jnp.concatenate([ukv[:, :, :QK_NOPE], jnp.zeros((KV_LORA, MLA_HEADS, QK_NOPE), F32)], axis=2)
    return {"win": win.astype(BF16), "qg": q_norm.reshape(1, -1), "wq": wq.astype(BF16),
            "kvg": kv_norm.reshape(1, -1), "wk": wk.reshape(KV_LORA, -1).astype(BF16),
            "wv": ukv[:, :, QK_NOPE:].reshape(KV_LORA, -1).astype(BF16)}


def _post_weights(wo, gf, w1, w3, w2):
    return {"wo": wo.astype(BF16), "gf": gf.reshape(1, -1), "w1": w1.astype(BF16),
            "w3": w3.astype(BF16), "w2": w2.astype(BF16)}


def _ssd_in_weights(gain, w_in, dt_bias):
    wd = jnp.pad(w_in[:, SSD_INNER + SSD_CONV_DIM:], ((0, 0), (0, LANES - 2 * SSD_HEADS)))
    wdh = wd.astype(BF16)
    wdl = (wd - wdh.astype(F32)).astype(BF16)
    db = jnp.pad(dt_bias.reshape(1, -1), ((0, 0), (0, LANES - 2 * SSD_HEADS)))
    return {"g": gain.reshape(1, -1), "wz": w_in[:, :SSD_INNER].astype(BF16),
            "wx": w_in[:, SSD_INNER:SSD_INNER + SSD_CONV_DIM].astype(BF16),
            "wdh": wdh, "wdl": wdl, "db": db}


def kernel(x, c, ctx, c_ctx, ada_w, ada_b, norm_mix, norm_ffn, ffn_w1, ffn_w3, ffn_w2,
           mla_w_in, mla_q_norm, mla_w_uq, mla_kv_norm, mla_w_ukv, mla_w_o,
           ssd_w_in, ssd_conv_w, ssd_conv_b, ssd_dt_bias, ssd_a_log, ssd_d, ssd_norm, ssd_w_o,
           final_norm):
    bsz, seq, _ = x.shape
    lc = ctx.shape[1]
    tm = 256

    cc = jnp.zeros((8, D_MODEL), F32).at[:bsz].set(c).at[bsz].set(c_ctx)
    mod = _ada(cc, ada_w, ada_b)
    mod_x = [mod[l, :bsz].reshape(bsz, 6, D_MODEL) for l in range(DEPTH)]
    mod_c = [mod[l, bsz:bsz + 1].reshape(1, 6, D_MODEL) for l in range(DEPTH)]

    wm = _mla_weights(mla_w_in[0], mla_q_norm[0], mla_w_uq[0], mla_kv_norm[0], mla_w_ukv[0])
    g0 = norm_mix[0].reshape(1, -1)
    q_c, k_c, v_c = _mla_proj(ctx, mod_c[0], g0, wm, _mla_tables(lc, False), tm)
    q_x, k_x, v_x = _mla_proj(x, mod_x[0], g0, wm, _mla_tables(seq, True), tm)
    o_x = _attn(q_x, [(k_c, v_c), (k_x, v_x)], tq=256, tk=512)
    o_c = _attn(q_c, [(k_c, v_c)], tq=lc, tk=512)
    wp0 = _post_weights(mla_w_o[0], norm_ffn[0], ffn_w1[0], ffn_w3[0], ffn_w2[0])
    x = _post(x, o_x, mod_x[0], wp0, tm)
    ctx = _post(ctx, o_c, mod_c[0], wp0, tm)

    ws = _ssd_in_weights(norm_mix[1], ssd_w_in[0], ssd_dt_bias[0])
    _, xbc_c, dt_c = _ssd_in(ctx, mod_c[1], ws, tm)
    z_x, xbc_x, dt_x = _ssd_in(x, mod_x[1], ws, tm)
    alog = jnp.pad(ssd_a_log[0].reshape(1, -1), ((0, 0), (0, LANES - 2 * SSD_HEADS)))
    dsk = jnp.repeat(ssd_d[0], SSD_HEADDIM).reshape(1, -1)
    y = _ssd(xbc_c, dt_c, xbc_x, dt_x, alog, dsk, ssd_conv_w[0], ssd_conv_b[0].reshape(1, -1))
    wp1 = _post_weights(ssd_w_o[0], norm_ffn[1], ffn_w1[1], ffn_w3[1], ffn_w2[1])
    return _post(x, (y, z_x, ssd_norm[0].reshape(1, -1)), mod_x[1], wp1, tm,
                 final_gain=final_norm.reshape(1, -1))
```

```python
import functools
import math

import numpy as np
import jax
import jax.numpy as jnp
from jax import lax
from jax.experimental import pallas as pl
from jax.experimental.pallas import tpu as pltpu

D_MODEL = 1024
DEPTH = 2
GRID_W = 64
EPS = 1e-6
MLA_HEADS = 16
Q_LORA = 512
KV_LORA = 256
QK_NOPE = 64
QK_ROPE = 32
V_HEAD = 64
ROPE_THETA = 10000.0
SSD_INNER = 2 * D_MODEL
SSD_HEADDIM = 64
SSD_HEADS = SSD_INNER // SSD_HEADDIM
SSD_GROUPS = 4
SSD_STATE = 128
SSD_CONV = 5
SSD_CONV_DIM = SSD_INNER + 2 * SSD_GROUPS * SSD_STATE
CHUNK = 128
FFN_HIDDEN = ((8 * D_MODEL // 3 + 255) // 256) * 256

LANES = 128
V7X_VMEM_BYTES = 64 * 1024 * 1024
VMEM_LIMIT_CAP = 56 * 1024 * 1024

F32 = jnp.float32
BF16 = jnp.bfloat16
HEAD_LANES = 2 * QK_NOPE
HEADS_PER_GROUP = SSD_HEADS // SSD_GROUPS
GROUP_W = HEADS_PER_GROUP * SSD_HEADDIM


def _cparams(semantics, vmem_bytes):
    return pltpu.CompilerParams(
        dimension_semantics=semantics,
        vmem_limit_bytes=int(min(VMEM_LIMIT_CAP, vmem_bytes)),
    )


def _const_spec(shape):
    nd = len(shape)
    return pl.BlockSpec(shape, lambda *_: (0,) * nd, pipeline_mode=pl.Buffered(1))


def _rms(x, g):
    return x * lax.rsqrt(jnp.mean(x * x, axis=-1, keepdims=True) + EPS) * g


def _silu(x):
    return x * (1.0 / (1.0 + jnp.exp(-x)))


def _dot(a, b):
    return jnp.dot(a, b, preferred_element_type=F32)


def _dot_nt(a, b):
    return lax.dot_general(a, b, (((1,), (1,)), ((), ())), preferred_element_type=F32)


def _dot_f32(a, b):
    return jnp.dot(a, b, preferred_element_type=F32, precision=lax.Precision.HIGHEST)


def _ada_kernel(c_ref, w_ref, b_ref, o_ref):
    s = _silu(c_ref[...])
    o_ref[...] = _dot_f32(s, w_ref[...]) + b_ref[...]


def _ada(cc, ada_w, ada_b):
    n_tiles = ada_w.shape[2] // D_MODEL
    return pl.pallas_call(
        _ada_kernel,
        grid=(DEPTH, n_tiles),
        in_specs=[
            pl.BlockSpec((8, D_MODEL), lambda l, j: (0, 0)),
            pl.BlockSpec((None, D_MODEL, D_MODEL), lambda l, j: (l, 0, j)),
            pl.BlockSpec((None, 1, D_MODEL), lambda l, j: (l, 0, j)),
        ],
        out_specs=pl.BlockSpec((None, 8, D_MODEL), lambda l, j: (l, 0, j)),
        out_shape=jax.ShapeDtypeStruct((DEPTH, 8, ada_w.shape[2]), F32),
        compiler_params=_cparams(("arbitrary", "arbitrary"), 24 << 20),
        name="ada",
    )(cc, ada_w, ada_b.reshape(DEPTH, 1, -1))


def _mod_spec(mod):
    if mod.shape[0] == 1:
        return pl.BlockSpec((None, 6, D_MODEL), lambda b, i: (0, 0, 0))
    return pl.BlockSpec((None, 6, D_MODEL), lambda b, i: (b, 0, 0))


def _mla_proj_kernel(x_ref, mod_ref, g_ref, win_ref, qg_ref, wq_ref, kvg_ref, wk_ref, wv_ref,
                     tq_ref, tkc_ref, tks_ref, q_ref, k_ref, v_ref):
    x = x_ref[...]
    h = _rms(x, g_ref[...]) * (1.0 + mod_ref[1:2, :]) + mod_ref[0:1, :]
    p = _dot(h.astype(BF16), win_ref[...])
    cq = p[:, :Q_LORA]
    ckv = p[:, Q_LORA:Q_LORA + KV_LORA]
    kr_a = p[:, Q_LORA + KV_LORA:Q_LORA + KV_LORA + LANES]
    kr_b = p[:, Q_LORA + KV_LORA + LANES:]
    cqn = _rms(cq, qg_ref[...]).astype(BF16)
    ckvn = _rms(ckv, kvg_ref[...]).astype(BF16)
    qf = _dot(cqn, wq_ref[...])
    kf = _dot(ckvn, wk_ref[...])
    kr = kr_a * tkc_ref[...] + kr_b * tks_ref[...]
    tq = tq_ref[...]
    for hd in range(MLA_HEADS):
        sl = slice(hd * HEAD_LANES, (hd + 1) * HEAD_LANES)
        q_ref[hd] = (qf[:, sl] * tq).astype(BF16)
        k_ref[hd] = (kf[:, sl] + kr).astype(BF16)
    v_ref[...] = _dot(ckvn, wv_ref[...]).astype(BF16)


def _mla_proj(x, mod, gain, w, tabs, tm):
    bsz, seq, _ = x.shape
    row = lambda n: pl.BlockSpec((None, tm, n), lambda b, i: (b, i, 0))
    heads = pl.BlockSpec((None, MLA_HEADS, tm, HEAD_LANES), lambda b, i: (b, 0, i, 0))
    tab = pl.BlockSpec((tm, LANES), lambda b, i: (i, 0))
    return pl.pallas_call(
        _mla_proj_kernel,
        grid=(bsz, seq // tm),
        in_specs=[
            row(D_MODEL), _mod_spec(mod), _const_spec((1, D_MODEL)),
            _const_spec(w["win"].shape), _const_spec((1, Q_LORA)), _const_spec(w["wq"].shape),
            _const_spec((1, KV_LORA)), _const_spec(w["wk"].shape), _const_spec(w["wv"].shape),
            tab, tab, tab,
        ],
        out_specs=[heads, heads, row(MLA_HEADS * V_HEAD)],
        out_shape=[
            jax.ShapeDtypeStruct((bsz, MLA_HEADS, seq, HEAD_LANES), BF16),
            jax.ShapeDtypeStruct((bsz, MLA_HEADS, seq, HEAD_LANES), BF16),
            jax.ShapeDtypeStruct((bsz, seq, MLA_HEADS * V_HEAD), BF16),
        ],
        compiler_params=_cparams(("arbitrary", "arbitrary"), 40 << 20),
        name="mla_proj",
    )(x, mod, gain, w["win"], w["qg"], w["wq"], w["kvg"], w["wk"], w["wv"],
      tabs["q"], tabs["kc"], tabs["ks"])


def _attn_kernel(*refs, n_kv, tq, tk):
    q_ref = refs[0]
    kv_refs = [(refs[1 + 2 * j], refs[2 + 2 * j]) for j in range(n_kv)]
    o_ref = refs[1 + 2 * n_kv]
    s_ref, m_ref, vm_ref = refs[2 + 2 * n_kv:]
    n_q = q_ref.shape[1] // tq
    chan = lax.broadcasted_iota(jnp.int32, (2 * V_HEAD, 1), 0)
    low_half = chan < V_HEAD
    chunks = []
    col = 0
    for k_ref, v_ref in kv_refs:
        rows = k_ref.shape[1]
        v_t = v_ref[...].astype(F32).T
        vm_ref[0, :, col:col + rows] = jnp.where(low_half, v_t, 1.0).astype(BF16)
        vm_ref[1, :, col:col + rows] = jnp.where(low_half, 1.0, v_t).astype(BF16)
        step = min(tk, rows)
        for r0 in range(0, rows, step):
            chunks.append((k_ref, r0, step, col + r0))
        col += rows

    def tile_rows(i):
        start = i * tq
        return pl.ds(start if isinstance(start, int) else pl.multiple_of(start, tq), tq)

    def scores(i, hh):
        qh = q_ref[hh, tile_rows(i), :]
        m = None
        for k_ref, r0, rows, c0 in chunks:
            s_t = _dot_nt(k_ref[hh, r0:r0 + rows, :], qh)
            s_ref[hh, c0:c0 + rows, :] = s_t
            cm = jnp.max(s_t, axis=0, keepdims=True)
            m = cm if m is None else jnp.maximum(m, cm)
        m_ref[hh] = jnp.broadcast_to(m, (8, tq))

    def softmax_pv(hh):
        acc = jnp.zeros((2 * V_HEAD, tq), F32)
        for _, _, rows, c0 in chunks:
            p_t = jnp.exp2(s_ref[hh, c0:c0 + rows, :] - m_ref[hh, 0:1, :]).astype(BF16)
            acc = acc + _dot(vm_ref[hh, :, c0:c0 + rows], p_t)
        return acc

    def tile(i, next_scores):
        scores(i, 1)
        acc0 = softmax_pv(0)
        if next_scores:
            scores(i + 1, 0)
        acc1 = softmax_pv(1)
        out_t = jnp.where(low_half, acc0 / acc0[V_HEAD:V_HEAD + 1, :], acc1 / acc1[0:1, :])
        o_ref[tile_rows(i), :] = out_t.T.astype(BF16)

    scores(0, 0)

    def body(i, carry):
        tile(i, True)
        return carry

    lax.fori_loop(0, n_q - 1, body, 0)
    tile(n_q - 1, False)


def _attn(q, kvs, tq, tk):
    bsz, _, lq, _ = q.shape
    n_pairs = MLA_HEADS // 2
    total = sum(k.shape[2] for k, _ in kvs)
    in_specs = [pl.BlockSpec((None, 2, lq, HEAD_LANES), lambda b, h: (b, h, 0, 0))]
    args = [q]
    for k, v in kvs:
        t = k.shape[2]
        in_specs.append(pl.BlockSpec((None, 2, t, HEAD_LANES), lambda b, h: (b, h, 0, 0)))
        in_specs.append(pl.BlockSpec((None, t, 2 * V_HEAD), lambda b, h: (b, 0, h)))
        args += [k, v]
    blocks = 2 * (2 * lq * HEAD_LANES + total * (2 * HEAD_LANES + 2 * V_HEAD) + lq * 2 * V_HEAD) * 2
    vmem = blocks + 2 * tq * (total + LANES) * 4 + 2 * total * 2 * V_HEAD * 2 + (16 << 20)
    return pl.pallas_call(
        functools.partial(_attn_kernel, n_kv=len(kvs), tq=tq, tk=tk),
        grid=(bsz, n_pairs),
        in_specs=in_specs,
        out_specs=pl.BlockSpec((None, lq, 2 * V_HEAD), lambda b, h: (b, 0, h)),
        out_shape=jax.ShapeDtypeStruct((bsz, lq, MLA_HEADS * V_HEAD), BF16),
        scratch_shapes=[pltpu.VMEM((2, total, tq), F32), pltpu.VMEM((2, 8, tq), F32),
                        pltpu.VMEM((2, 2 * V_HEAD, total), BF16)],
        compiler_params=_cparams(("arbitrary", "arbitrary"), vmem),
        name="attn",
    )(*args)


def _post_kernel(*refs, ssd, final):
    it = iter(refs)
    x_ref = next(it)
    if ssd:
        y_ref, z_ref, ng_ref = next(it), next(it), next(it)
    else:
        a_ref = next(it)
    mod_ref, wo_ref, gf_ref, w1_ref, w3_ref, w2_ref = (next(it) for _ in range(6))
    fg_ref = next(it) if final else None
    o_ref = next(it)

    if ssd:
        gated = y_ref[...].astype(F32) * _silu(z_ref[...].astype(F32))
        a = _rms(gated, ng_ref[...]).astype(BF16)
    else:
        a = a_ref[...]
    x1 = x_ref[...] + mod_ref[2:3, :] * _dot(a, wo_ref[...])
    hx = (_rms(x1, gf_ref[...]) * (1.0 + mod_ref[4:5, :]) + mod_ref[3:4, :]).astype(BF16)
    u = _dot(hx, w1_ref[...])
    w = _dot(hx, w3_ref[...])
    t = (_silu(u) * w).astype(BF16)
    x2 = x1 + mod_ref[5:6, :] * _dot(t, w2_ref[...])
    if final:
        x2 = _rms(x2, fg_ref[...])
    o_ref[...] = x2


def _post(x, mix_in, mod, w, tm, final_gain=None):
    bsz, seq, _ = x.shape
    ssd = isinstance(mix_in, tuple)
    row = lambda n: pl.BlockSpec((None, tm, n), lambda b, i: (b, i, 0))
    in_specs = [row(D_MODEL)]
    args = [x]
    if ssd:
        y, z, ng = mix_in
        in_specs += [row(SSD_INNER), row(SSD_INNER), _const_spec((1, SSD_INNER))]
        args += [y, z, ng]
    else:
        in_specs.append(row(mix_in.shape[2]))
        args.append(mix_in)
    in_specs += [_mod_spec(mod), _const_spec(w["wo"].shape), _const_spec((1, D_MODEL)),
                 _const_spec(w["w1"].shape), _const_spec(w["w3"].shape), _const_spec(w["w2"].shape)]
    args += [mod, w["wo"], w["gf"], w["w1"], w["w3"], w["w2"]]
    if final_gain is not None:
        in_specs.append(_const_spec((1, D_MODEL)))
        args.append(final_gain)
    return pl.pallas_call(
        functools.partial(_post_kernel, ssd=ssd, final=final_gain is not None),
        grid=(bsz, seq // tm),
        in_specs=in_specs,
        out_specs=row(D_MODEL),
        out_shape=jax.ShapeDtypeStruct((bsz, seq, D_MODEL), F32),
        compiler_params=_cparams(("arbitrary", "arbitrary"), VMEM_LIMIT_CAP),
        name="post_ssd" if ssd else "post_mla",
    )(*args)


def _ssd_in_kernel(x_ref, mod_ref, g_ref, wz_ref, wx_ref, wdh_ref, wdl_ref, db_ref,
                   z_ref, xbc_ref, dt_ref):
    h = _rms(x_ref[...], g_ref[...]) * (1.0 + mod_ref[1:2, :]) + mod_ref[0:1, :]
    hb = h.astype(BF16)
    z_ref[...] = _dot(hb, wz_ref[...]).astype(BF16)
    xbc_ref[...] = _dot(hb, wx_ref[...]).astype(BF16)
    hl = (h - hb.astype(F32)).astype(BF16)
    raw = _dot(hb, wdh_ref[...]) + _dot(hl, wdh_ref[...]) + _dot(hb, wdl_ref[...])
    raw = raw + db_ref[...]
    dt_ref[...] = jnp.maximum(raw, 0.0) + jnp.log(1.0 + jnp.exp(-jnp.abs(raw)))


def _ssd_in(x, mod, w, tm):
    bsz, seq, _ = x.shape
    row = lambda n: pl.BlockSpec((None, tm, n), lambda b, i: (b, i, 0))
    return pl.pallas_call(
        _ssd_in_kernel,
        grid=(bsz, seq // tm),
        in_specs=[row(D_MODEL), _mod_spec(mod), _const_spec((1, D_MODEL)),
                  _const_spec(w["wz"].shape), _const_spec(w["wx"].shape),
                  _const_spec(w["wdh"].shape), _const_spec(w["wdl"].shape),
                  _const_spec((1, LANES))],
        out_specs=[row(SSD_INNER), row(SSD_CONV_DIM), row(LANES)],
        out_shape=[jax.ShapeDtypeStruct((bsz, seq, SSD_INNER), BF16),
                   jax.ShapeDtypeStruct((bsz, seq, SSD_CONV_DIM), BF16),
                   jax.ShapeDtypeStruct((bsz, seq, LANES), F32)],
        compiler_params=_cparams(("arbitrary", "arbitrary"), 48 << 20),
        name="ssd_in",
    )(x, mod, w["g"], w["wz"], w["wx"], w["wdh"], w["wdl"], w["db"])


CONV_PAD = 16
LOG2E = math.log2(math.e)


def _ssd_kernel(xc_ref, bc_ref, cc_ref, dtc_ref, xx_ref, bx_ref, cx_ref, dtx_ref,
                alog_ref, dsk_ref, cwx_ref, cwb_ref, cwc_ref, cbx_ref, cbb_ref, cbc_ref,
                y_ref, s_ref, w_ref, bt_ref, e_ref, cd_ref,
                cum_ref, cumt_ref, dtt_ref, btr_ref, ux_ref, ub_ref, uc_ref, shift_ref):
    g = pl.program_id(1)
    n_ctx = xc_ref.shape[0] // CHUNK
    n_lat = xx_ref.shape[0] // CHUNK
    a_all = jnp.broadcast_to(-jnp.exp(alog_ref[...]), (8, LANES))
    qi = lax.broadcasted_iota(jnp.int32, (CHUNK, CHUNK), 0)
    ki = lax.broadcasted_iota(jnp.int32, (CHUNK, CHUNK), 1)
    lane = lax.broadcasted_iota(jnp.int32, (1, LANES), 1)
    low_half = lane < SSD_HEADDIM
    n_pair = HEADS_PER_GROUP // 2

    def pair_cols(v, pr):
        return jnp.where(low_half, v[:, 2 * pr:2 * pr + 1], v[:, 2 * pr + 1:2 * pr + 2])

    def split3(v):
        hi = v.astype(BF16)
        r1 = v - hi.astype(F32)
        mid = r1.astype(BF16)
        return hi, mid, (r1 - mid.astype(F32)).astype(BF16)

    def lane_bcast(v, col):
        return jnp.broadcast_to(v[:, col:col + 1], (v.shape[0], LANES))

    sel_r = lax.broadcasted_iota(jnp.int32, (CHUNK, CHUNK + 2 * CONV_PAD), 0)
    sel_c = lax.broadcasted_iota(jnp.int32, (CHUNK, CHUNK + 2 * CONV_PAD), 1)
    for j in range(SSD_CONV):
        hit = sel_c == sel_r + (CONV_PAD - SSD_CONV // 2 + j)
        shift_ref[j * CHUNK:(j + 1) * CHUNK, :] = jnp.where(hit, 1.0, 0.0).astype(BF16)

    def window(raw_ref, r0, first, last):
        pad = jnp.zeros((CONV_PAD, raw_ref.shape[1]), BF16)
        if first and last:
            return jnp.concatenate([pad, raw_ref[0:CHUNK, :], pad], axis=0)
        if first:
            return jnp.concatenate([pad, raw_ref[0:CHUNK + CONV_PAD, :]], axis=0)
        if last:
            return jnp.concatenate([raw_ref[r0 - CONV_PAD:r0 + CHUNK, :], pad], axis=0)
        return raw_ref[pl.ds(pl.multiple_of(r0 - CONV_PAD, CONV_PAD), CHUNK + 2 * CONV_PAD), :]

    def scan_sums(x_ref, b_ref, c_ref, dt_ref, r0, chunk_id, first, last):
        grow = pl.ds(chunk_id * CHUNK if isinstance(chunk_id, int)
                     else pl.multiple_of(chunk_id * CHUNK, CHUNK), CHUNK)
        win = jnp.concatenate([window(r, r0, first, last) for r in (x_ref, b_ref, c_ref)], axis=1)
        taps = _dot(shift_ref[...], win)
        cw = jnp.concatenate([cwx_ref[...], cwb_ref[...], cwc_ref[...]], axis=1)
        acc = jnp.concatenate([cbx_ref[...], cbb_ref[...], cbc_ref[...]], axis=1)
        for j in range(SSD_CONV):
            acc = acc + taps[j * CHUNK:(j + 1) * CHUNK, :] * cw[j:j + 1, :]
        u = _silu(acc).astype(BF16)
        ub = u[:, GROUP_W:GROUP_W + SSD_STATE]
        ux_ref[grow, :] = u[:, :GROUP_W]
        ub_ref[grow, :] = ub
        uc_ref[grow, :] = u[:, GROUP_W + SSD_STATE:]
        btr_ref[chunk_id] = ub.astype(F32).T
        dt = dt_ref[pl.ds(r0, CHUNK), :]
        for direction in range(2):
            causal = (ki <= qi) if direction == 0 else (ki >= qi)
            shift = (LANES - (direction * SSD_HEADS + g * HEADS_PER_GROUP)) & (LANES - 1)
            dts = pltpu.roll(dt, shift, 1)
            a = dts * pltpu.roll(a_all, shift, 1)[0:1, :]
            tri = jnp.where(causal, 1.0, 0.0).astype(BF16)
            c = sum(_dot(tri, piece) for piece in split3(a)) * LOG2E
            cum_ref[direction, chunk_id] = c
            cumt_ref[direction, chunk_id] = c.T[0:8, :]
            dtt_ref[direction, chunk_id] = dts.T[0:8, :]

    def prepare(chunk_id, direction, slot, emit):
        grow = pl.ds(chunk_id * CHUNK if isinstance(chunk_id, int)
                     else pl.multiple_of(chunk_id * CHUNK, CHUNK), CHUNK)
        causal = (ki <= qi) if direction == 0 else (ki >= qi)
        c = cum_ref[direction, chunk_id]
        c_t = cumt_ref[direction, chunk_id]
        dt_t = dtt_ref[direction, chunk_id]
        end = CHUNK - 1 if direction == 0 else 0
        to_end_t = jnp.exp2(c_t[:, end:end + 1] - c_t) * dt_t
        cd = jnp.exp2(c[end:end + 1, :])
        cd_ref[direction, slot] = jnp.broadcast_to(
            jnp.concatenate([pair_cols(cd, pr) for pr in range(n_pair)], axis=1), (8, GROUP_W))
        b_t = btr_ref[chunk_id]
        if emit:
            cb = _dot_nt(uc_ref[grow, :], ub_ref[grow, :])
        for pr in range(n_pair):
            heads = (2 * pr, 2 * pr + 1)
            wide = slice(pr * 2 * LANES, (pr + 1) * 2 * LANES)
            bt_ref[direction, slot, :, wide] = jnp.concatenate(
                [(b_t * to_end_t[hd:hd + 1, :]).astype(BF16) for hd in heads], axis=1)
            if emit:
                ws, e_b = [], []
                for hd in heads:
                    cq = lane_bcast(c, hd)
                    dec = jnp.exp2(jnp.where(causal, cq - c_t[hd:hd + 1, :], -jnp.inf))
                    ws.append((cb * dec * dt_t[hd:hd + 1, :]).astype(BF16))
                    e_b.append(jnp.exp2(cq))
                w_ref[direction, slot, :, wide] = jnp.concatenate(ws, axis=1)
                e_ref[direction, slot, :, pr * LANES:(pr + 1) * LANES] = jnp.where(low_half, e_b[0], e_b[1])

    def consume(chunk_id, direction, slot, emit):
        grow = pl.ds(chunk_id * CHUNK if isinstance(chunk_id, int)
                     else pl.multiple_of(chunk_id * CHUNK, CHUNK), CHUNK)
        st_ref = s_ref.at[direction]
        x = ux_ref[grow, :]
        if emit:
            y_off = _dot(uc_ref[grow, :], st_ref[...].astype(BF16))
        ys, upds = [], []
        for pr in range(n_pair):
            ps = slice(pr * LANES, (pr + 1) * LANES)
            wide = slice(pr * 2 * LANES, (pr + 1) * 2 * LANES)
            xp = x[:, ps]
            zero = jnp.zeros_like(xp)
            rhs = jnp.concatenate([jnp.where(low_half, xp, zero),
                                   jnp.where(low_half, zero, xp)], axis=0)
            upds.append(_dot(bt_ref[direction, slot, :, wide], rhs))
            if emit:
                ys.append(_dot(w_ref[direction, slot, :, wide], rhs)
                          + e_ref[direction, slot, :, ps] * y_off[:, ps])
        st_ref[...] = st_ref[...] * cd_ref[direction, slot, 0:1, :] + jnp.concatenate(upds, axis=1)
        return (jnp.concatenate(ys, axis=1), x) if emit else (None, None)

    dsk = dsk_ref[...]
    s_ref[...] = jnp.zeros_like(s_ref)
    for ci in range(n_ctx):
        scan_sums(xc_ref, bc_ref, cc_ref, dtc_ref, ci * CHUNK, ci, ci == 0, ci == n_ctx - 1)

    assert n_lat >= 2
    scan_sums(xx_ref, bx_ref, cx_ref, dtx_ref, 0, n_ctx, True, False)

    def sums_body(j, carry):
        scan_sums(xx_ref, bx_ref, cx_ref, dtx_ref, pl.multiple_of(j * CHUNK, CHUNK), n_ctx + j,
                  False, False)
        return carry

    lax.fori_loop(1, n_lat - 1, sums_body, 0, unroll=2)
    scan_sums(xx_ref, bx_ref, cx_ref, dtx_ref, (n_lat - 1) * CHUNK, n_ctx + n_lat - 1, False, True)

    for ci in range(n_ctx):
        for direction, cj in ((0, ci), (1, n_ctx - 1 - ci)):
            prepare(cj, direction, 0, False)
            consume(cj, direction, 0, False)

    def chunk_of(i, direction):
        return i if direction == 0 else n_lat - 1 - i

    def chunk_row(i, direction):
        r0 = chunk_of(i, direction) * CHUNK
        return r0 if isinstance(r0, int) else pl.multiple_of(r0, CHUNK)

    def stage_prepare(i):
        for direction in range(2):
            prepare(n_ctx + chunk_of(i, direction), direction, i & 1, True)

    def stage_consume(i, first_touch):
        for direction in range(2):
            rows = pl.ds(chunk_row(i, direction), CHUNK)
            y, x = consume(n_ctx + chunk_of(i, direction), direction, i & 1, True)
            if direction == 0:
                y = y + dsk * x.astype(F32)
            if not first_touch:
                y = y + y_ref[rows, :].astype(F32)
            y_ref[rows, :] = y.astype(BF16)

    def body(i, carry, first_touch):
        stage_prepare(i + 1)
        stage_consume(i, first_touch)
        return carry

    assert n_lat % 2 == 0
    stage_prepare(0)
    lax.fori_loop(0, n_lat // 2, functools.partial(body, first_touch=True), 0)
    lax.fori_loop(n_lat // 2, n_lat - 1, functools.partial(body, first_touch=False), 0)
    stage_consume(n_lat - 1, False)


def _ssd(uc, dtc, ux, dtx, alog, dsk, conv_w, conv_b):
    bsz, lc, _ = uc.shape
    lx = ux.shape[1]
    n_chunks = (lc + lx) // CHUNK
    b_blk0 = SSD_INNER // SSD_STATE
    c_blk0 = b_blk0 + SSD_GROUPS
    chan_specs = [(GROUP_W, lambda b, g: (0, g)),
                  (SSD_STATE, lambda b, g: (0, b_blk0 + g)),
                  (SSD_STATE, lambda b, g: (0, c_blk0 + g))]

    def seg_specs(rows):
        return [pl.BlockSpec((None, rows, GROUP_W), lambda b, g: (b, 0, g)),
                pl.BlockSpec((None, rows, SSD_STATE), lambda b, g: (b, 0, b_blk0 + g)),
                pl.BlockSpec((None, rows, SSD_STATE), lambda b, g: (b, 0, c_blk0 + g)),
                pl.BlockSpec((None, rows, LANES), lambda b, g: (b, 0, 0))]

    return pl.pallas_call(
        _ssd_kernel,
        grid=(bsz, SSD_GROUPS),
        in_specs=seg_specs(lc) + seg_specs(lx) + [
            pl.BlockSpec((1, LANES), lambda b, g: (0, 0)),
            pl.BlockSpec((1, GROUP_W), lambda b, g: (0, g))]
        + [pl.BlockSpec((SSD_CONV, wd), f) for wd, f in chan_specs]
        + [pl.BlockSpec((1, wd), f) for wd, f in chan_specs],
        out_specs=pl.BlockSpec((None, lx, GROUP_W), lambda b, g: (b, 0, g)),
        out_shape=jax.ShapeDtypeStruct((bsz, lx, SSD_INNER), BF16),
        scratch_shapes=[
            pltpu.VMEM((2, SSD_STATE, GROUP_W), F32),
            pltpu.VMEM((2, 2, CHUNK, 2 * GROUP_W), BF16),
            pltpu.VMEM((2, 2, SSD_STATE, 2 * GROUP_W), BF16),
            pltpu.VMEM((2, 2, CHUNK, GROUP_W), F32),
            pltpu.VMEM((2, 2, 8, GROUP_W), F32),
            pltpu.VMEM((2, n_chunks, CHUNK, LANES), F32),
            pltpu.VMEM((2, n_chunks, 8, CHUNK), F32),
            pltpu.VMEM((2, n_chunks, 8, CHUNK), F32),
            pltpu.VMEM((n_chunks, SSD_STATE, CHUNK), F32),
            pltpu.VMEM((lc + lx, GROUP_W), BF16),
            pltpu.VMEM((lc + lx, SSD_STATE), BF16),
            pltpu.VMEM((lc + lx, SSD_STATE), BF16),
            pltpu.VMEM((SSD_CONV * CHUNK, CHUNK + 2 * CONV_PAD), BF16),
        ],
        compiler_params=_cparams(("arbitrary", "arbitrary"), 54 << 20),
        name="ssd",
    )(uc, uc, uc, dtc, ux, ux, ux, dtx, alog, dsk, conv_w, conv_w, conv_w, conv_b, conv_b, conv_b)


def _rope_partner_index():
    j = np.arange(QK_ROPE)
    return np.where(j % 16 < 8, j + 8, j - 8)


def _rope_tables(seq):
    t = np.arange(seq)
    pos = np.stack([t // GRID_W, t % GRID_W], axis=1).astype(np.float32)
    j = np.arange(QK_ROPE)
    inv = ROPE_THETA ** (-(jnp.arange(8, dtype=F32)) * 2.0 / 16.0)
    ang = jnp.asarray(pos)[:, j // 16] * inv[j % 8][None, :]
    sign = jnp.asarray(np.where(j % 16 < 8, -1.0, 1.0).astype(np.float32))
    return jnp.cos(ang), jnp.sin(ang) * sign


def _mla_tables(seq, rotate):
    scale = math.log2(math.e) / math.sqrt(QK_NOPE + QK_ROPE)
    if rotate:
        cos, sin = _rope_tables(seq)
    else:
        cos, sin = jnp.ones((seq, QK_ROPE), F32), jnp.zeros((seq, QK_ROPE), F32)
    one = jnp.ones((seq, QK_NOPE), F32)
    zero = jnp.zeros((seq, QK_NOPE), F32)
    return {"q": jnp.concatenate([one, cos, sin], axis=1) * scale,
            "kc": jnp.concatenate([zero, cos, cos], axis=1),
            "ks": jnp.concatenate([zero, sin, sin], axis=1)}


def _mla_weights(w_in, q_norm, w_uq, kv_norm, w_ukv):
    partner = _rope_partner_index()
    kr = w_in[:, Q_LORA + KV_LORA:]
    krp = kr[:, partner]
    zero = jnp.zeros((D_MODEL, QK_NOPE), F32)
    win = jnp.concatenate([w_in[:, :Q_LORA + KV_LORA], zero, kr, kr, zero, krp, krp], axis=1)
    uq = w_uq.reshape(Q_LORA, MLA_HEADS, QK_NOPE + QK_ROPE)
    rope = uq[:, :, QK_NOPE:]
    wq = jnp.concatenate([uq, rope[:, :, partner]], axis=2).reshape(Q_LORA, MLA_HEADS * HEAD_LANES)
    ukv = w_ukv.reshape(KV_LORA, MLA_HEADS, QK_NOPE + V_HEAD)
    wk = jnp.concatenate([ukv[:, :, :QK_NOPE], jnp.zeros((KV_LORA, MLA_HEADS, QK_NOPE), F32)], axis=2)
    return {"win": win.astype(BF16), "qg": q_norm.reshape(1, -1), "wq": wq.astype(BF16),
            "kvg": kv_norm.reshape(1, -1), "wk": wk.reshape(KV_LORA, -1).astype(BF16),
            "wv": ukv[:, :, QK_NOPE:].reshape(KV_LORA, -1).astype(BF16)}


def _post_weights(wo, gf, w1, w3, w2):
    return {"wo": wo.astype(BF16), "gf": gf.reshape(1, -1), "w1": w1.astype(BF16),
            "w3": w3.astype(BF16), "w2": w2.astype(BF16)}


def _ssd_in_weights(gain, w_in, dt_bias):
    wd = jnp.pad(w_in[:, SSD_INNER + SSD_CONV_DIM:], ((0, 0), (0, LANES - 2 * SSD_HEADS)))
    wdh = wd.astype(BF16)
    wdl = (wd - wdh.astype(F32)).astype(BF16)
    db = jnp.pad(dt_bias.reshape(1, -1), ((0, 0), (0, LANES - 2 * SSD_HEADS)))
    return {"g": gain.reshape(1, -1), "wz": w_in[:, :SSD_INNER].astype(BF16),
            "wx": w_in[:, SSD_INNER:SSD_INNER + SSD_CONV_DIM].astype(BF16),
            "wdh": wdh, "wdl": wdl, "db": db}


def kernel(x, c, ctx, c_ctx, ada_w, ada_b, norm_mix, norm_ffn, ffn_w1, ffn_w3, ffn_w2,
           mla_w_in, mla_q_norm, mla_w_uq, mla_kv_norm, mla_w_ukv, mla_w_o,
           ssd_w_in, ssd_conv_w, ssd_conv_b, ssd_dt_bias, ssd_a_log, ssd_d, ssd_norm, ssd_w_o,
           final_norm):
    bsz, seq, _ = x.shape
    lc = ctx.shape[1]
    tm = 256

    cc = jnp.zeros((8, D_MODEL), F32).at[:bsz].set(c).at[bsz].set(c_ctx)
    mod = _ada(cc, ada_w, ada_b)
    mod_x = [mod[l, :bsz].reshape(bsz, 6, D_MODEL) for l in range(DEPTH)]
    mod_c = [mod[l, bsz:bsz + 1].reshape(1, 6, D_MODEL) for l in range(DEPTH)]

    wm = _mla_weights(mla_w_in[0], mla_q_norm[0], mla_w_uq[0], mla_kv_norm[0], mla_w_ukv[0])
    g0 = norm_mix[0].reshape(1, -1)
    q_c, k_c, v_c = _mla_proj(ctx, mod_c[0], g0, wm, _mla_tables(lc, False), tm)
    q_x, k_x, v_x = _mla_proj(x, mod_x[0], g0, wm, _mla_tables(seq, True), tm)
    o_x = _attn(q_x, [(k_c, v_c), (k_x, v_x)], tq=256, tk=512)
    o_c = _attn(q_c, [(k_c, v_c)], tq=lc, tk=512)
    wp0 = _post_weights(mla_w_o[0], norm_ffn[0], ffn_w1[0], ffn_w3[0], ffn_w2[0])
    x = _post(x, o_x, mod_x[0], wp0, tm)
    ctx = _post(ctx, o_c, mod_c[0], wp0, tm)

    ws = _ssd_in_weights(norm_mix[1], ssd_w_in[0], ssd_dt_bias[0])
    _, xbc_c, dt_c = _ssd_in(ctx, mod_c[1], ws, tm)
    z_x, xbc_x, dt_x = _ssd_in(x, mod_x[1], ws, tm)
    alog = jnp.pad(ssd_a_log[0].reshape(1, -1), ((0, 0), (0, LANES - 2 * SSD_HEADS)))
    dsk = jnp.repeat(ssd_d[0], SSD_HEADDIM).reshape(1, -1)
    y = _ssd(xbc_c, dt_c, xbc_x, dt_x, alog, dsk, ssd_conv_w[0], ssd_conv_b[0].reshape(1, -1))
    wp1 = _post_weights(ssd_w_o[0], norm_ffn[1], ffn_w1[1], ffn_w3[1], ffn_w2[1])
    return _post(x, (y, z_x, ssd_norm[0].reshape(1, -1)), mod_x[1], wp1, tm,
                 final_gain=final_norm.reshape(1, -1))
```

```python
import functools
import math

import numpy as np
import jax
import jax.numpy as jnp
from jax import lax
from jax.experimental import pallas as pl
from jax.experimental.pallas import tpu as pltpu

D_MODEL = 1024
DEPTH = 2
GRID_W = 64
EPS = 1e-6
MLA_HEADS = 16
Q_LORA = 512
KV_LORA = 256
QK_NOPE = 64
QK_ROPE = 32
V_HEAD = 64
ROPE_THETA = 10000.0
SSD_INNER = 2 * D_MODEL
SSD_HEADDIM = 64
SSD_HEADS = SSD_INNER // SSD_HEADDIM
SSD_GROUPS = 4
SSD_STATE = 128
SSD_CONV = 5
SSD_CONV_DIM = SSD_INNER + 2 * SSD_GROUPS * SSD_STATE
CHUNK = 128
FFN_HIDDEN = ((8 * D_MODEL // 3 + 255) // 256) * 256

LANES = 128
V7X_VMEM_BYTES = 64 * 1024 * 1024
VMEM_LIMIT_CAP = 56 * 1024 * 1024

F32 = jnp.float32
BF16 = jnp.bfloat16
HEAD_LANES = 2 * QK_NOPE
HEADS_PER_GROUP = SSD_HEADS // SSD_GROUPS
GROUP_W = HEADS_PER_GROUP * SSD_HEADDIM


def _cparams(semantics, vmem_bytes):
    return pltpu.CompilerParams(
        dimension_semantics=semantics,
        vmem_limit_bytes=int(min(VMEM_LIMIT_CAP, vmem_bytes)),
    )


def _const_spec(shape):
    nd = len(shape)
    return pl.BlockSpec(shape, lambda *_: (0,) * nd, pipeline_mode=pl.Buffered(1))


def _rms(x, g):
    return x * lax.rsqrt(jnp.mean(x * x, axis=-1, keepdims=True) + EPS) * g


def _silu(x):
    return x * (1.0 / (1.0 + jnp.exp(-x)))


def _dot(a, b):
    return jnp.dot(a, b, preferred_element_type=F32)


def _dot_nt(a, b):
    return lax.dot_general(a, b, (((1,), (1,)), ((), ())), preferred_element_type=F32)


def _dot_f32(a, b):
    return jnp.dot(a, b, preferred_element_type=F32, precision=lax.Precision.HIGHEST)


def _ada_kernel(c_ref, w_ref, b_ref, o_ref):
    s = _silu(c_ref[...])
    o_ref[...] = _dot_f32(s, w_ref[...]) + b_ref[...]


def _ada(cc, ada_w, ada_b):
    n_tiles = ada_w.shape[2] // D_MODEL
    return pl.pallas_call(
        _ada_kernel,
        grid=(DEPTH, n_tiles),
        in_specs=[
            pl.BlockSpec((8, D_MODEL), lambda l, j: (0, 0)),
            pl.BlockSpec((None, D_MODEL, D_MODEL), lambda l, j: (l, 0, j)),
            pl.BlockSpec((None, 1, D_MODEL), lambda l, j: (l, 0, j)),
        ],
        out_specs=pl.BlockSpec((None, 8, D_MODEL), lambda l, j: (l, 0, j)),
        out_shape=jax.ShapeDtypeStruct((DEPTH, 8, ada_w.shape[2]), F32),
        compiler_params=_cparams(("arbitrary", "arbitrary"), 24 << 20),
        name="ada",
    )(cc, ada_w, ada_b.reshape(DEPTH, 1, -1))


def _mod_spec(mod):
    if mod.shape[0] == 1:
        return pl.BlockSpec((None, 6, D_MODEL), lambda b, i: (0, 0, 0))
    return pl.BlockSpec((None, 6, D_MODEL), lambda b, i: (b, 0, 0))


def _mla_proj_kernel(x_ref, mod_ref, g_ref, win_ref, qg_ref, wq_ref, kvg_ref, wk_ref, wv_ref,
                     tq_ref, tkc_ref, tks_ref, q_ref, k_ref, v_ref):
    x = x_ref[...]
    h = _rms(x, g_ref[...]) * (1.0 + mod_ref[1:2, :]) + mod_ref[0:1, :]
    p = _dot(h.astype(BF16), win_ref[...])
    cq = p[:, :Q_LORA]
    ckv = p[:, Q_LORA:Q_LORA + KV_LORA]
    kr_a = p[:, Q_LORA + KV_LORA:Q_LORA + KV_LORA + LANES]
    kr_b = p[:, Q_LORA + KV_LORA + LANES:]
    cqn = _rms(cq, qg_ref[...]).astype(BF16)
    ckvn = _rms(ckv, kvg_ref[...]).astype(BF16)
    qf = _dot(cqn, wq_ref[...])
    kf = _dot(ckvn, wk_ref[...])
    kr = kr_a * tkc_ref[...] + kr_b * tks_ref[...]
    tq = tq_ref[...]
    for hd in range(MLA_HEADS):
        sl = slice(hd * HEAD_LANES, (hd + 1) * HEAD_LANES)
        q_ref[hd] = (qf[:, sl] * tq).astype(BF16)
        k_ref[hd] = (kf[:, sl] + kr).astype(BF16)
    v_ref[...] = _dot(ckvn, wv_ref[...]).astype(BF16)


def _mla_proj(x, mod, gain, w, tabs, tm):
    bsz, seq, _ = x.shape
    row = lambda n: pl.BlockSpec((None, tm, n), lambda b, i: (b, i, 0))
    heads = pl.BlockSpec((None, MLA_HEADS, tm, HEAD_LANES), lambda b, i: (b, 0, i, 0))
    tab = pl.BlockSpec((tm, LANES), lambda b, i: (i, 0))
    return pl.pallas_call(
        _mla_proj_kernel,
        grid=(bsz, seq // tm),
        in_specs=[
            row(D_MODEL), _mod_spec(mod), _const_spec((1, D_MODEL)),
            _const_spec(w["win"].shape), _const_spec((1, Q_LORA)), _const_spec(w["wq"].shape),
            _const_spec((1, KV_LORA)), _const_spec(w["wk"].shape), _const_spec(w["wv"].shape),
            tab, tab, tab,
        ],
        out_specs=[heads, heads, row(MLA_HEADS * V_HEAD)],
        out_shape=[
            jax.ShapeDtypeStruct((bsz, MLA_HEADS, seq, HEAD_LANES), BF16),
            jax.ShapeDtypeStruct((bsz, MLA_HEADS, seq, HEAD_LANES), BF16),
            jax.ShapeDtypeStruct((bsz, seq, MLA_HEADS * V_HEAD), BF16),
        ],
        compiler_params=_cparams(("arbitrary", "arbitrary"), 40 << 20),
        name="mla_proj",
    )(x, mod, gain, w["win"], w["qg"], w["wq"], w["kvg"], w["wk"], w["wv"],
      tabs["q"], tabs["kc"], tabs["ks"])


def _attn_kernel(*refs, n_kv, tq, tk):
    q_ref = refs[0]
    kv_refs = [(refs[1 + 2 * j], refs[2 + 2 * j]) for j in range(n_kv)]
    o_ref = refs[1 + 2 * n_kv]
    s_ref, m_ref, vm_ref = refs[2 + 2 * n_kv:]
    n_q = q_ref.shape[1] // tq
    chan = lax.broadcasted_iota(jnp.int32, (2 * V_HEAD, 1), 0)
    low_half = chan < V_HEAD
    chunks = []
    col = 0
    for k_ref, v_ref in kv_refs:
        rows = k_ref.shape[1]
        v_t = v_ref[...].astype(F32).T
        vm_ref[0, :, col:col + rows] = jnp.where(low_half, v_t, 1.0).astype(BF16)
        vm_ref[1, :, col:col + rows] = jnp.where(low_half, 1.0, v_t).astype(BF16)
        step = min(tk, rows)
        for r0 in range(0, rows, step):
            chunks.append((k_ref, r0, step, col + r0))
        col += rows

    def tile_rows(i):
        start = i * tq
        return pl.ds(start if isinstance(start, int) else pl.multiple_of(start, tq), tq)

    def scores(i, hh):
        qh = q_ref[hh, tile_rows(i), :]
        m = None
        for k_ref, r0, rows, c0 in chunks:
            s_t = _dot_nt(k_ref[hh, r0:r0 + rows, :], qh)
            s_ref[hh, c0:c0 + rows, :] = s_t
            cm = jnp.max(s_t, axis=0, keepdims=True)
            m = cm if m is None else jnp.maximum(m, cm)
        m_ref[hh] = jnp.broadcast_to(m, (8, tq))

    def softmax_pv(hh):
        acc = jnp.zeros((2 * V_HEAD, tq), F32)
        for _, _, rows, c0 in chunks:
            p_t = jnp.exp2(s_ref[hh, c0:c0 + rows, :] - m_ref[hh, 0:1, :]).astype(BF16)
            acc = acc + _dot(vm_ref[hh, :, c0:c0 + rows], p_t)
        return acc

    def tile(i, next_scores):
        scores(i, 1)
        acc0 = softmax_pv(0)
        if next_scores:
            scores(i + 1, 0)
        acc1 = softmax_pv(1)
        out_t = jnp.where(low_half, acc0 / acc0[V_HEAD:V_HEAD + 1, :], acc1 / acc1[0:1, :])
        o_ref[tile_rows(i), :] = out_t.T.astype(BF16)

    scores(0, 0)

    def body(i, carry):
        tile(i, True)
        return carry

    lax.fori_loop(0, n_q - 1, body, 0)
    tile(n_q - 1, False)


def _attn(q, kvs, tq, tk):
    bsz, _, lq, _ = q.shape
    n_pairs = MLA_HEADS // 2
    total = sum(k.shape[2] for k, _ in kvs)
    in_specs = [pl.BlockSpec((None, 2, lq, HEAD_LANES), lambda b, h: (b, h, 0, 0))]
    args = [q]
    for k, v in kvs:
        t = k.shape[2]
        in_specs.append(pl.BlockSpec((None, 2, t, HEAD_LANES), lambda b, h: (b, h, 0, 0)))
        in_specs.append(pl.BlockSpec((None, t, 2 * V_HEAD), lambda b, h: (b, 0, h)))
        args += [k, v]
    blocks = 2 * (2 * lq * HEAD_LANES + total * (2 * HEAD_LANES + 2 * V_HEAD) + lq * 2 * V_HEAD) * 2
    vmem = blocks + 2 * tq * (total + LANES) * 4 + 2 * total * 2 * V_HEAD * 2 + (16 << 20)
    return pl.pallas_call(
        functools.partial(_attn_kernel, n_kv=len(kvs), tq=tq, tk=tk),
        grid=(bsz, n_pairs),
        in_specs=in_specs,
        out_specs=pl.BlockSpec((None, lq, 2 * V_HEAD), lambda b, h: (b, 0, h)),
        out_shape=jax.ShapeDtypeStruct((bsz, lq, MLA_HEADS * V_HEAD), BF16),
        scratch_shapes=[pltpu.VMEM((2, total, tq), F32), pltpu.VMEM((2, 8, tq), F32),
                        pltpu.VMEM((2, 2 * V_HEAD, total), BF16)],
        compiler_params=_cparams(("arbitrary", "arbitrary"), vmem),
        name="attn",
    )(*args)


def _resid_ffn(mix, x_ref, mod_ref, wo_ref, gf_ref, w1_ref, w3_ref, w2_ref, fg_ref, o_ref, sub):
    tiles = [slice(r0, r0 + sub) for r0 in range(0, x_ref.shape[0], sub)]
    a = [mix(rows) for rows in tiles]
    x1 = [x_ref[rows, :] + mod_ref[2:3, :] * _dot(ak, wo_ref[...]) for rows, ak in zip(tiles, a)]
    hx = [(_rms(xk, gf_ref[...]) * (1.0 + mod_ref[4:5, :]) + mod_ref[3:4, :]).astype(BF16) for xk in x1]
    t = [(_silu(_dot(hk, w1_ref[...])) * _dot(hk, w3_ref[...])).astype(BF16) for hk in hx]
    for rows, xk, tk_ in zip(tiles, x1, t):
        x2 = xk + mod_ref[5:6, :] * _dot(tk_, w2_ref[...])
        if fg_ref is not None:
            x2 = _rms(x2, fg_ref[...])
        o_ref[rows, :] = x2


def _post_mla_kernel(x_ref, a_ref, mod_ref, wo_ref, gf_ref, w1_ref, w3_ref, w2_ref, o_ref, *, sub):
    _resid_ffn(lambda rows: a_ref[rows, :], x_ref, mod_ref, wo_ref, gf_ref, w1_ref, w3_ref, w2_ref,
               None, o_ref, sub)


def _post_ssd_kernel(x_ref, y_ref, z_ref, ng_ref, mod_ref, wo_ref, gf_ref, w1_ref, w3_ref, w2_ref,
                     fg_ref, o_ref, *, sub):
    def gated_norm(rows):
        gated = y_ref[rows, :].astype(F32) * _silu(z_ref[rows, :].astype(F32))
        return _rms(gated, ng_ref[...]).astype(BF16)

    _resid_ffn(gated_norm, x_ref, mod_ref, wo_ref, gf_ref, w1_ref, w3_ref, w2_ref, fg_ref, o_ref, sub)


def _post_weight_specs(w):
    return [_const_spec(w["wo"].shape), _const_spec((1, D_MODEL)), _const_spec(w["w1"].shape),
            _const_spec(w["w3"].shape), _const_spec(w["w2"].shape)]


def _post_mla(x, a, mod, w, tm, sub=256):
    bsz, seq, _ = x.shape
    row = lambda n: pl.BlockSpec((None, tm, n), lambda b, i: (b, i, 0))
    return pl.pallas_call(
        functools.partial(_post_mla_kernel, sub=min(sub, tm)),
        grid=(bsz, seq // tm),
        in_specs=[row(D_MODEL), row(a.shape[2]), _mod_spec(mod)] + _post_weight_specs(w),
        out_specs=row(D_MODEL),
        out_shape=jax.ShapeDtypeStruct((bsz, seq, D_MODEL), F32),
        compiler_params=_cparams(("arbitrary", "arbitrary"), VMEM_LIMIT_CAP),
        name="post_mla",
    )(x, a, mod, w["wo"], w["gf"], w["w1"], w["w3"], w["w2"])


def _post_ssd(x, y, z, norm_gain, mod, w, final_gain, tm, sub=256):
    bsz, seq, _ = x.shape
    row = lambda n: pl.BlockSpec((None, tm, n), lambda b, i: (b, i, 0))
    return pl.pallas_call(
        functools.partial(_post_ssd_kernel, sub=min(sub, tm)),
        grid=(bsz, seq // tm),
        in_specs=[row(D_MODEL), row(SSD_INNER), row(SSD_INNER), _const_spec((1, SSD_INNER)),
                  _mod_spec(mod)] + _post_weight_specs(w) + [_const_spec((1, D_MODEL))],
        out_specs=row(D_MODEL),
        out_shape=jax.ShapeDtypeStruct((bsz, seq, D_MODEL), F32),
        compiler_params=_cparams(("arbitrary", "arbitrary"), VMEM_LIMIT_CAP),
        name="post_ssd",
    )(x, y, z, norm_gain, mod, w["wo"], w["gf"], w["w1"], w["w3"], w["w2"], final_gain)


def _ssd_in_kernel(x_ref, mod_ref, g_ref, wz_ref, wx_ref, wdh_ref, wdl_ref, db_ref,
                   z_ref, xbc_ref, dt_ref):
    h = _rms(x_ref[...], g_ref[...]) * (1.0 + mod_ref[1:2, :]) + mod_ref[0:1, :]
    hb = h.astype(BF16)
    z_ref[...] = _dot(hb, wz_ref[...]).astype(BF16)
    xbc_ref[...] = _dot(hb, wx_ref[...]).astype(BF16)
    hl = (h - hb.astype(F32)).astype(BF16)
    raw = _dot(hb, wdh_ref[...]) + _dot(hl, wdh_ref[...]) + _dot(hb, wdl_ref[...])
    raw = raw + db_ref[...]
    dt_ref[...] = jnp.maximum(raw, 0.0) + jnp.log(1.0 + jnp.exp(-jnp.abs(raw)))


def _ssd_in(x, mod, w, tm):
    bsz, seq, _ = x.shape
    row = lambda n: pl.BlockSpec((None, tm, n), lambda b, i: (b, i, 0))
    return pl.pallas_call(
        _ssd_in_kernel,
        grid=(bsz, seq // tm),
        in_specs=[row(D_MODEL), _mod_spec(mod), _const_spec((1, D_MODEL)),
                  _const_spec(w["wz"].shape), _const_spec(w["wx"].shape),
                  _const_spec(w["wdh"].shape), _const_spec(w["wdl"].shape),
                  _const_spec((1, LANES))],
        out_specs=[row(SSD_INNER), row(SSD_CONV_DIM), row(LANES)],
        out_shape=[jax.ShapeDtypeStruct((bsz, seq, SSD_INNER), BF16),
                   jax.ShapeDtypeStruct((bsz, seq, SSD_CONV_DIM), BF16),
                   jax.ShapeDtypeStruct((bsz, seq, LANES), F32)],
        compiler_params=_cparams(("arbitrary", "arbitrary"), 48 << 20),
        name="ssd_in",
    )(x, mod, w["g"], w["wz"], w["wx"], w["wdh"], w["wdl"], w["db"])


CONV_PAD = 16
LOG2E = math.log2(math.e)


def _ssd_kernel(xc_ref, bc_ref, cc_ref, dtc_ref, xx_ref, bx_ref, cx_ref, dtx_ref,
                alog_ref, dsk_ref, cwx_ref, cwb_ref, cwc_ref, cbx_ref, cbb_ref, cbc_ref,
                y_ref, s_ref, w_ref, bt_ref, e_ref, cd_ref,
                cum_ref, cumt_ref, dtt_ref, btr_ref, ux_ref, ub_ref, uc_ref, shift_ref):
    g = pl.program_id(1)
    n_ctx = xc_ref.shape[0] // CHUNK
    n_lat = xx_ref.shape[0] // CHUNK
    a_all = jnp.broadcast_to(-jnp.exp(alog_ref[...]), (8, LANES))
    qi = lax.broadcasted_iota(jnp.int32, (CHUNK, CHUNK), 0)
    ki = lax.broadcasted_iota(jnp.int32, (CHUNK, CHUNK), 1)
    lane = lax.broadcasted_iota(jnp.int32, (1, LANES), 1)
    low_half = lane < SSD_HEADDIM
    n_pair = HEADS_PER_GROUP // 2

    def pair_cols(v, pr):
        return jnp.where(low_half, v[:, 2 * pr:2 * pr + 1], v[:, 2 * pr + 1:2 * pr + 2])

    def split3(v):
        hi = v.astype(BF16)
        r1 = v - hi.astype(F32)
        mid = r1.astype(BF16)
        return hi, mid, (r1 - mid.astype(F32)).astype(BF16)

    def lane_bcast(v, col):
        return jnp.broadcast_to(v[:, col:col + 1], (v.shape[0], LANES))

    sel_r = lax.broadcasted_iota(jnp.int32, (CHUNK, CHUNK + 2 * CONV_PAD), 0)
    sel_c = lax.broadcasted_iota(jnp.int32, (CHUNK, CHUNK + 2 * CONV_PAD), 1)
    shifted_taps = [j for j in range(SSD_CONV) if j != SSD_CONV // 2]
    for n, j in enumerate(shifted_taps):
        hit = sel_c == sel_r + (CONV_PAD - SSD_CONV // 2 + j)
        shift_ref[n * CHUNK:(n + 1) * CHUNK, :] = jnp.where(hit, 1.0, 0.0).astype(BF16)

    def window(raw_ref, r0, first, last):
        pad = jnp.zeros((CONV_PAD, raw_ref.shape[1]), BF16)
        if first and last:
            return jnp.concatenate([pad, raw_ref[0:CHUNK, :], pad], axis=0)
        if first:
            return jnp.concatenate([pad, raw_ref[0:CHUNK + CONV_PAD, :]], axis=0)
        if last:
            return jnp.concatenate([raw_ref[r0 - CONV_PAD:r0 + CHUNK, :], pad], axis=0)
        return raw_ref[pl.ds(pl.multiple_of(r0 - CONV_PAD, CONV_PAD), CHUNK + 2 * CONV_PAD), :]

    def scan_sums(x_ref, b_ref, c_ref, dt_ref, r0, chunk_id, first, last):
        grow = pl.ds(chunk_id * CHUNK if isinstance(chunk_id, int)
                     else pl.multiple_of(chunk_id * CHUNK, CHUNK), CHUNK)
        win = jnp.concatenate([window(r, r0, first, last) for r in (x_ref, b_ref, c_ref)], axis=1)
        taps = _dot(shift_ref[...], win)
        cw = jnp.concatenate([cwx_ref[...], cwb_ref[...], cwc_ref[...]], axis=1)
        acc = jnp.concatenate([cbx_ref[...], cbb_ref[...], cbc_ref[...]], axis=1)
        mid = SSD_CONV // 2
        acc = acc + win[CONV_PAD:CONV_PAD + CHUNK, :].astype(F32) * cw[mid:mid + 1, :]
        for n, j in enumerate(shifted_taps):
            acc = acc + taps[n * CHUNK:(n + 1) * CHUNK, :] * cw[j:j + 1, :]
        u = _silu(acc).astype(BF16)
        ub = u[:, GROUP_W:GROUP_W + SSD_STATE]
        ux_ref[grow, :] = u[:, :GROUP_W]
        ub_ref[grow, :] = ub
        uc_ref[grow, :] = u[:, GROUP_W + SSD_STATE:]
        btr_ref[chunk_id] = ub.astype(F32).T
        dt = dt_ref[pl.ds(r0, CHUNK), :]
        for direction in range(2):
            causal = (ki <= qi) if direction == 0 else (ki >= qi)
            shift = (LANES - (direction * SSD_HEADS + g * HEADS_PER_GROUP)) & (LANES - 1)
            dts = pltpu.roll(dt, shift, 1)
            a = dts * pltpu.roll(a_all, shift, 1)[0:1, :]
            tri = jnp.where(causal, 1.0, 0.0).astype(BF16)
            r = _dot(tri, jnp.concatenate(split3(a), axis=1))
            c = (r[:, :LANES] + r[:, LANES:2 * LANES] + r[:, 2 * LANES:]) * LOG2E
            cum_ref[direction, chunk_id] = c
            cumt_ref[direction, chunk_id] = c.T[0:8, :]
            dtt_ref[direction, chunk_id] = dts.T[0:8, :]

    def prepare(chunk_id, direction, slot, emit):
        grow = pl.ds(chunk_id * CHUNK if isinstance(chunk_id, int)
                     else pl.multiple_of(chunk_id * CHUNK, CHUNK), CHUNK)
        causal = (ki <= qi) if direction == 0 else (ki >= qi)
        c = cum_ref[direction, chunk_id]
        c_t = cumt_ref[direction, chunk_id]
        dt_t = dtt_ref[direction, chunk_id]
        end = CHUNK - 1 if direction == 0 else 0
        to_end_t = jnp.exp2(c_t[:, end:end + 1] - c_t) * dt_t
        cd = jnp.exp2(c[end:end + 1, :])
        cd_ref[direction, slot] = jnp.broadcast_to(
            jnp.concatenate([pair_cols(cd, pr) for pr in range(n_pair)], axis=1), (8, GROUP_W))
        b_t = btr_ref[chunk_id]
        if emit:
            cb = _dot_nt(uc_ref[grow, :], ub_ref[grow, :])
        for pr in range(n_pair):
            heads = (2 * pr, 2 * pr + 1)
            wide = slice(pr * 2 * LANES, (pr + 1) * 2 * LANES)
            bt_ref[direction, slot, :, wide] = jnp.concatenate(
                [(b_t * to_end_t[hd:hd + 1, :]).astype(BF16) for hd in heads], axis=1)
            if emit:
                ws, e_b = [], []
                for hd in heads:
                    cq = lane_bcast(c, hd)
                    dec = jnp.exp2(jnp.where(causal, cq - c_t[hd:hd + 1, :], -jnp.inf))
                    ws.append((cb * dec * dt_t[hd:hd + 1, :]).astype(BF16))
                    e_b.append(jnp.exp2(cq))
                w_ref[direction, slot, :, wide] = jnp.concatenate(ws, axis=1)
                e_ref[direction, slot, :, pr * LANES:(pr + 1) * LANES] = jnp.where(low_half, e_b[0], e_b[1])

    def consume(chunk_id, direction, slot, emit):
        grow = pl.ds(chunk_id * CHUNK if isinstance(chunk_id, int)
                     else pl.multiple_of(chunk_id * CHUNK, CHUNK), CHUNK)
        st_ref = s_ref.at[direction]
        x = ux_ref[grow, :]
        if emit:
            y_off = _dot(uc_ref[grow, :], st_ref[...].astype(BF16))
        ys, upds = [], []
        for pr in range(n_pair):
            ps = slice(pr * LANES, (pr + 1) * LANES)
            wide = slice(pr * 2 * LANES, (pr + 1) * 2 * LANES)
            xp = x[:, ps]
            zero = jnp.zeros_like(xp)
            rhs = jnp.concatenate([jnp.where(low_half, xp, zero),
                                   jnp.where(low_half, zero, xp)], axis=0)
            upds.append(_dot(bt_ref[direction, slot, :, wide], rhs))
            if emit:
                ys.append(_dot(w_ref[direction, slot, :, wide], rhs)
                          + e_ref[direction, slot, :, ps] * y_off[:, ps])
        st_ref[...] = st_ref[...] * cd_ref[direction, slot, 0:1, :] + jnp.concatenate(upds, axis=1)
        return (jnp.concatenate(ys, axis=1), x) if emit else (None, None)

    dsk = dsk_ref[...]
    s_ref[...] = jnp.zeros_like(s_ref)
    for ci in range(n_ctx):
        scan_sums(xc_ref, bc_ref, cc_ref, dtc_ref, ci * CHUNK, ci, ci == 0, ci == n_ctx - 1)

    assert n_lat >= 2
    scan_sums(xx_ref, bx_ref, cx_ref, dtx_ref, 0, n_ctx, True, False)

    def sums_body(j, carry):
        scan_sums(xx_ref, bx_ref, cx_ref, dtx_ref, pl.multiple_of(j * CHUNK, CHUNK), n_ctx + j,
                  False, False)
        return carry

    lax.fori_loop(1, n_lat - 1, sums_body, 0, unroll=2)
    scan_sums(xx_ref, bx_ref, cx_ref, dtx_ref, (n_lat - 1) * CHUNK, n_ctx + n_lat - 1, False, True)

    for ci in range(n_ctx):
        for direction, cj in ((0, ci), (1, n_ctx - 1 - ci)):
            prepare(cj, direction, 0, False)
            consume(cj, direction, 0, False)

    def chunk_of(i, direction):
        return i if direction == 0 else n_lat - 1 - i

    def chunk_row(i, direction):
        r0 = chunk_of(i, direction) * CHUNK
        return r0 if isinstance(r0, int) else pl.multiple_of(r0, CHUNK)

    def stage_prepare(i):
        for direction in range(2):
            prepare(n_ctx + chunk_of(i, direction), direction, i & 1, True)

    def stage_consume(i, first_touch):
        for direction in range(2):
            rows = pl.ds(chunk_row(i, direction), CHUNK)
            y, x = consume(n_ctx + chunk_of(i, direction), direction, i & 1, True)
            if direction == 0:
                y = y + dsk * x.astype(F32)
            if not first_touch:
                y = y + y_ref[rows, :].astype(F32)
            y_ref[rows, :] = y.astype(BF16)

    def body(i, carry, first_touch):
        stage_prepare(i + 1)
        stage_consume(i, first_touch)
        return carry

    assert n_lat % 2 == 0
    stage_prepare(0)
    lax.fori_loop(0, n_lat // 2, functools.partial(body, first_touch=True), 0)
    lax.fori_loop(n_lat // 2, n_lat - 1, functools.partial(body, first_touch=False), 0)
    stage_consume(n_lat - 1, False)


def _ssd(uc, dtc, ux, dtx, alog, dsk, conv_w, conv_b):
    bsz, lc, _ = uc.shape
    lx = ux.shape[1]
    n_chunks = (lc + lx) // CHUNK
    b_blk0 = SSD_INNER // SSD_STATE
    c_blk0 = b_blk0 + SSD_GROUPS
    chan_specs = [(GROUP_W, lambda b, g: (0, g)),
                  (SSD_STATE, lambda b, g: (0, b_blk0 + g)),
                  (SSD_STATE, lambda b, g: (0, c_blk0 + g))]

    def seg_specs(rows):
        return [pl.BlockSpec((None, rows, GROUP_W), lambda b, g: (b, 0, g)),
                pl.BlockSpec((None, rows, SSD_STATE), lambda b, g: (b, 0, b_blk0 + g)),
                pl.BlockSpec((None, rows, SSD_STATE), lambda b, g: (b, 0, c_blk0 + g)),
                pl.BlockSpec((None, rows, LANES), lambda b, g: (b, 0, 0))]

    return pl.pallas_call(
        _ssd_kernel,
        grid=(bsz, SSD_GROUPS),
        in_specs=seg_specs(lc) + seg_specs(lx) + [
            pl.BlockSpec((1, LANES), lambda b, g: (0, 0)),
            pl.BlockSpec((1, GROUP_W), lambda b, g: (0, g))]
        + [pl.BlockSpec((SSD_CONV, wd), f) for wd, f in chan_specs]
        + [pl.BlockSpec((1, wd), f) for wd, f in chan_specs],
        out_specs=pl.BlockSpec((None, lx, GROUP_W), lambda b, g: (b, 0, g)),
        out_shape=jax.ShapeDtypeStruct((bsz, lx, SSD_INNER), BF16),
        scratch_shapes=[
            pltpu.VMEM((2, SSD_STATE, GROUP_W), F32),
            pltpu.VMEM((2, 2, CHUNK, 2 * GROUP_W), BF16),
            pltpu.VMEM((2, 2, SSD_STATE, 2 * GROUP_W), BF16),
            pltpu.VMEM((2, 2, CHUNK, GROUP_W), F32),
            pltpu.VMEM((2, 2, 8, GROUP_W), F32),
            pltpu.VMEM((2, n_chunks, CHUNK, LANES), F32),
            pltpu.VMEM((2, n_chunks, 8, CHUNK), F32),
            pltpu.VMEM((2, n_chunks, 8, CHUNK), F32),
            pltpu.VMEM((n_chunks, SSD_STATE, CHUNK), F32),
            pltpu.VMEM((lc + lx, GROUP_W), BF16),
            pltpu.VMEM((lc + lx, SSD_STATE), BF16),
            pltpu.VMEM((lc + lx, SSD_STATE), BF16),
            pltpu.VMEM(((SSD_CONV - 1) * CHUNK, CHUNK + 2 * CONV_PAD), BF16),
        ],
        compiler_params=_cparams(("arbitrary", "arbitrary"), 54 << 20),
        name="ssd",
    )(uc, uc, uc, dtc, ux, ux, ux, dtx, alog, dsk, conv_w, conv_w, conv_w, conv_b, conv_b, conv_b)


def _rope_partner_index():
    j = np.arange(QK_ROPE)
    return np.where(j % 16 < 8, j + 8, j - 8)


def _rope_tables(seq):
    t = np.arange(seq)
    pos = np.stack([t // GRID_W, t % GRID_W], axis=1).astype(np.float32)
    j = np.arange(QK_ROPE)
    inv = ROPE_THETA ** (-(jnp.arange(8, dtype=F32)) * 2.0 / 16.0)
    ang = jnp.asarray(pos)[:, j // 16] * inv[j % 8][None, :]
    sign = jnp.asarray(np.where(j % 16 < 8, -1.0, 1.0).astype(np.float32))
    return jnp.cos(ang), jnp.sin(ang) * sign


def _mla_tables(seq, rotate):
    scale = math.log2(math.e) / math.sqrt(QK_NOPE + QK_ROPE)
    if rotate:
        cos, sin = _rope_tables(seq)
    else:
        cos, sin = jnp.ones((seq, QK_ROPE), F32), jnp.zeros((seq, QK_ROPE), F32)
    one = jnp.ones((seq, QK_NOPE), F32)
    zero = jnp.zeros((seq, QK_NOPE), F32)
    return {"q": jnp.concatenate([one, cos, sin], axis=1) * scale,
            "kc": jnp.concatenate([zero, cos, cos], axis=1),
            "ks": jnp.concatenate([zero, sin, sin], axis=1)}


def _mla_weights(w_in, q_norm, w_uq, kv_norm, w_ukv):
    partner = _rope_partner_index()
    kr = w_in[:, Q_LORA + KV_LORA:]
    krp = kr[:, partner]
    zero = jnp.zeros((D_MODEL, QK_NOPE), F32)
    win = jnp.concatenate([w_in[:, :Q_LORA + KV_LORA], zero, kr, kr, zero, krp, krp], axis=1)
    uq = w_uq.reshape(Q_LORA, MLA_HEADS, QK_NOPE + QK_ROPE)
    rope = uq[:, :, QK_NOPE:]
    wq = jnp.concatenate([uq, rope[:, :, partner]], axis=2).reshape(Q_LORA, MLA_HEADS * HEAD_LANES)
    ukv = w_ukv.reshape(KV_LORA, MLA_HEADS, QK_NOPE + V_HEAD)
    wk = jnp.concatenate([ukv[:, :, :QK_NOPE], jnp.zeros((KV_LORA, MLA_HEADS, QK_NOPE), F32)], axis=2)
    return {"win": win.astype(BF16), "qg": q_norm.reshape(1, -1), "wq": wq.astype(BF16),
            "kvg": kv_norm.reshape(1, -1), "wk": wk.reshape(KV_LORA, -1).astype(BF16),
            "wv": ukv[:, :, QK_NOPE:].reshape(KV_LORA, -1).astype(BF16)}


def _post_weights(wo, gf, w1, w3, w2):
    return {"wo": wo.astype(BF16), "gf": gf.reshape(1, -1), "w1": w1.astype(BF16),
            "w3": w3.astype(BF16), "w2": w2.astype(BF16)}


def _ssd_in_weights(gain, w_in, dt_bias):
    wd = jnp.pad(w_in[:, SSD_INNER + SSD_CONV_DIM:], ((0, 0), (0, LANES - 2 * SSD_HEADS)))
    wdh = wd.astype(BF16)
    wdl = (wd - wdh.astype(F32)).astype(BF16)
    db = jnp.pad(dt_bias.reshape(1, -1), ((0, 0), (0, LANES - 2 * SSD_HEADS)))
    return {"g": gain.reshape(1, -1), "wz": w_in[:, :SSD_INNER].astype(BF16),
            "wx": w_in[:, SSD_INNER:SSD_INNER + SSD_CONV_DIM].astype(BF16),
            "wdh": wdh, "wdl": wdl, "db": db}


def kernel(x, c, ctx, c_ctx, ada_w, ada_b, norm_mix, norm_ffn, ffn_w1, ffn_w3, ffn_w2,
           mla_w_in, mla_q_norm, mla_w_uq, mla_kv_norm, mla_w_ukv, mla_w_o,
           ssd_w_in, ssd_conv_w, ssd_conv_b, ssd_dt_bias, ssd_a_log, ssd_d, ssd_norm, ssd_w_o,
           final_norm):
    bsz, seq, _ = x.shape
    lc = ctx.shape[1]
    tm = 256

    cc = jnp.zeros((8, D_MODEL), F32).at[:bsz].set(c).at[bsz].set(c_ctx)
    mod = _ada(cc, ada_w, ada_b)
    mod_x = [mod[l, :bsz].reshape(bsz, 6, D_MODEL) for l in range(DEPTH)]
    mod_c = [mod[l, bsz:bsz + 1].reshape(1, 6, D_MODEL) for l in range(DEPTH)]

    wm = _mla_weights(mla_w_in[0], mla_q_norm[0], mla_w_uq[0], mla_kv_norm[0], mla_w_ukv[0])
    g0 = norm_mix[0].reshape(1, -1)
    q_c, k_c, v_c = _mla_proj(ctx, mod_c[0], g0, wm, _mla_tables(lc, False), tm)
    q_x, k_x, v_x = _mla_proj(x, mod_x[0], g0, wm, _mla_tables(seq, True), tm)
    o_x = _attn(q_x, [(k_c, v_c), (k_x, v_x)], tq=256, tk=512)
    o_c = _attn(q_c, [(k_c, v_c)], tq=lc, tk=512)
    wp0 = _post_weights(mla_w_o[0], norm_ffn[0], ffn_w1[0], ffn_w3[0], ffn_w2[0])
    x = _post_mla(x, o_x, mod_x[0], wp0, 2 * tm)
    ctx = _post_mla(ctx, o_c, mod_c[0], wp0, tm)

    ws = _ssd_in_weights(norm_mix[1], ssd_w_in[0], ssd_dt_bias[0])
    _, xbc_c, dt_c = _ssd_in(ctx, mod_c[1], ws, tm)
    z_x, xbc_x, dt_x = _ssd_in(x, mod_x[1], ws, tm)
    alog = jnp.pad(ssd_a_log[0].reshape(1, -1), ((0, 0), (0, LANES - 2 * SSD_HEADS)))
    dsk = jnp.repeat(ssd_d[0], SSD_HEADDIM).reshape(1, -1)
    y = _ssd(xbc_c, dt_c, xbc_x, dt_x, alog, dsk, ssd_conv_w[0], ssd_conv_b[0].reshape(1, -1))
    wp1 = _post_weights(ssd_w_o[0], norm_ffn[1], ffn_w1[1], ffn_w3[1], ffn_w2[1])
    return _post_ssd(x, y, z_x, ssd_norm[0].reshape(1, -1), mod_x[1], wp1,
                     final_norm.reshape(1, -1), 2 * tm)
```

```python
import functools
import math

import numpy as np
import jax
import jax.numpy as jnp
from jax import lax
from jax.experimental import pallas as pl
from jax.experimental.pallas import tpu as pltpu

D_MODEL = 1024
DEPTH = 2
GRID_W = 64
EPS = 1e-6
MLA_HEADS = 16
Q_LORA = 512
KV_LORA = 256
QK_NOPE = 64
QK_ROPE = 32
V_HEAD = 64
ROPE_THETA = 10000.0
SSD_INNER = 2 * D_MODEL
SSD_HEADDIM = 64
SSD_HEADS = SSD_INNER // SSD_HEADDIM
SSD_GROUPS = 4
SSD_STATE = 128
SSD_CONV = 5
SSD_CONV_DIM = SSD_INNER + 2 * SSD_GROUPS * SSD_STATE
CHUNK = 128
FFN_HIDDEN = ((8 * D_MODEL // 3 + 255) // 256) * 256

LANES = 128
V7X_VMEM_BYTES = 64 * 1024 * 1024
VMEM_LIMIT_CAP = 56 * 1024 * 1024

F32 = jnp.float32
BF16 = jnp.bfloat16
HEAD_LANES = 2 * QK_NOPE
HEADS_PER_GROUP = SSD_HEADS // SSD_GROUPS
GROUP_W = HEADS_PER_GROUP * SSD_HEADDIM


def _cparams(semantics, vmem_bytes):
    return pltpu.CompilerParams(
        dimension_semantics=semantics,
        vmem_limit_bytes=int(min(VMEM_LIMIT_CAP, vmem_bytes)),
    )


def _const_spec(shape):
    nd = len(shape)
    return pl.BlockSpec(shape, lambda *_: (0,) * nd, pipeline_mode=pl.Buffered(1))


def _rms(x, g):
    return x * lax.rsqrt(jnp.mean(x * x, axis=-1, keepdims=True) + EPS) * g


def _silu(x):
    return x * (1.0 / (1.0 + jnp.exp(-x)))


def _dot(a, b):
    return jnp.dot(a, b, preferred_element_type=F32)


def _dot_nt(a, b):
    return lax.dot_general(a, b, (((1,), (1,)), ((), ())), preferred_element_type=F32)


def _dot_f32(a, b):
    return jnp.dot(a, b, preferred_element_type=F32, precision=lax.Precision.HIGHEST)


def _ada_kernel(c_ref, w_ref, b_ref, o_ref):
    s = _silu(c_ref[...])
    o_ref[...] = _dot_f32(s, w_ref[...]) + b_ref[...]


def _ada(cc, ada_w, ada_b):
    n_tiles = ada_w.shape[2] // D_MODEL
    return pl.pallas_call(
        _ada_kernel,
        grid=(DEPTH, n_tiles),
        in_specs=[
            pl.BlockSpec((8, D_MODEL), lambda l, j: (0, 0)),
            pl.BlockSpec((None, D_MODEL, D_MODEL), lambda l, j: (l, 0, j)),
            pl.BlockSpec((None, 1, D_MODEL), lambda l, j: (l, 0, j)),
        ],
        out_specs=pl.BlockSpec((None, 8, D_MODEL), lambda l, j: (l, 0, j)),
        out_shape=jax.ShapeDtypeStruct((DEPTH, 8, ada_w.shape[2]), F32),
        compiler_params=_cparams(("arbitrary", "arbitrary"), 24 << 20),
        name="ada",
    )(cc, ada_w, ada_b.reshape(DEPTH, 1, -1))


def _mod_spec(mod):
    if mod.shape[0] == 1:
        return pl.BlockSpec((None, 6, D_MODEL), lambda b, i: (0, 0, 0))
    return pl.BlockSpec((None, 6, D_MODEL), lambda b, i: (b, 0, 0))


def _mla_proj_kernel(x_ref, mod_ref, g_ref, win_ref, qg_ref, wq_ref, kvg_ref, wk_ref, wv_ref,
                     tq_ref, tkc_ref, tks_ref, q_ref, k_ref, v_ref):
    x = x_ref[...]
    h = _rms(x, g_ref[...]) * (1.0 + mod_ref[1:2, :]) + mod_ref[0:1, :]
    p = _dot(h.astype(BF16), win_ref[...])
    cq = p[:, :Q_LORA]
    ckv = p[:, Q_LORA:Q_LORA + KV_LORA]
    kr_a = p[:, Q_LORA + KV_LORA:Q_LORA + KV_LORA + LANES]
    kr_b = p[:, Q_LORA + KV_LORA + LANES:]
    cqn = _rms(cq, qg_ref[...]).astype(BF16)
    ckvn = _rms(ckv, kvg_ref[...]).astype(BF16)
    qf = _dot(cqn, wq_ref[...])
    kf = _dot(ckvn, wk_ref[...])
    kr = kr_a * tkc_ref[...] + kr_b * tks_ref[...]
    tq = tq_ref[...]
    for hd in range(MLA_HEADS):
        sl = slice(hd * HEAD_LANES, (hd + 1) * HEAD_LANES)
        q_ref[hd] = (qf[:, sl] * tq).astype(BF16)
        k_ref[hd] = (kf[:, sl] + kr).astype(BF16)
    v_ref[...] = _dot(ckvn, wv_ref[...]).astype(BF16)


def _mla_proj(x, mod, gain, w, tabs, tm):
    bsz, seq, _ = x.shape
    row = lambda n: pl.BlockSpec((None, tm, n), lambda b, i: (b, i, 0))
    heads = pl.BlockSpec((None, MLA_HEADS, tm, HEAD_LANES), lambda b, i: (b, 0, i, 0))
    tab = pl.BlockSpec((tm, LANES), lambda b, i: (i, 0))
    return pl.pallas_call(
        _mla_proj_kernel,
        grid=(bsz, seq // tm),
        in_specs=[
            row(D_MODEL), _mod_spec(mod), _const_spec((1, D_MODEL)),
            _const_spec(w["win"].shape), _const_spec((1, Q_LORA)), _const_spec(w["wq"].shape),
            _const_spec((1, KV_LORA)), _const_spec(w["wk"].shape), _const_spec(w["wv"].shape),
            tab, tab, tab,
        ],
        out_specs=[heads, heads, row(MLA_HEADS * V_HEAD)],
        out_shape=[
            jax.ShapeDtypeStruct((bsz, MLA_HEADS, seq, HEAD_LANES), BF16),
            jax.ShapeDtypeStruct((bsz, MLA_HEADS, seq, HEAD_LANES), BF16),
            jax.ShapeDtypeStruct((bsz, seq, MLA_HEADS * V_HEAD), BF16),
        ],
        compiler_params=_cparams(("arbitrary", "arbitrary"), 40 << 20),
        name="mla_proj",
    )(x, mod, gain, w["win"], w["qg"], w["wq"], w["kvg"], w["wk"], w["wv"],
      tabs["q"], tabs["kc"], tabs["ks"])


def _attn_kernel(*refs, n_kv, tq, tk):
    q_ref = refs[0]
    kv_refs = [(refs[1 + 2 * j], refs[2 + 2 * j]) for j in range(n_kv)]
    o_ref = refs[1 + 2 * n_kv]
    s_ref, m_ref, vm_ref = refs[2 + 2 * n_kv:]
    n_q = q_ref.shape[1] // tq
    chan = lax.broadcasted_iota(jnp.int32, (2 * V_HEAD, 1), 0)
    low_half = chan < V_HEAD
    chunks = []
    col = 0
    for k_ref, v_ref in kv_refs:
        rows = k_ref.shape[1]
        v_t = v_ref[...].astype(F32).T
        vm_ref[0, :, col:col + rows] = jnp.where(low_half, v_t, 1.0).astype(BF16)
        vm_ref[1, :, col:col + rows] = jnp.where(low_half, 1.0, v_t).astype(BF16)
        step = min(tk, rows)
        for r0 in range(0, rows, step):
            chunks.append((k_ref, r0, step, col + r0))
        col += rows

    def tile_rows(i):
        start = i * tq
        return pl.ds(start if isinstance(start, int) else pl.multiple_of(start, tq), tq)

    def scores(i, hh):
        qh = q_ref[hh, tile_rows(i), :]
        m = None
        for k_ref, r0, rows, c0 in chunks:
            s_t = _dot_nt(k_ref[hh, r0:r0 + rows, :], qh)
            s_ref[hh, c0:c0 + rows, :] = s_t
            cm = jnp.max(s_t, axis=0, keepdims=True)
            m = cm if m is None else jnp.maximum(m, cm)
        m_ref[hh] = jnp.broadcast_to(m, (8, tq))

    def softmax_pv(hh):
        acc = jnp.zeros((2 * V_HEAD, tq), F32)
        for _, _, rows, c0 in chunks:
            p_t = jnp.exp2(s_ref[hh, c0:c0 + rows, :] - m_ref[hh, 0:1, :]).astype(BF16)
            acc = acc + _dot(vm_ref[hh, :, c0:c0 + rows], p_t)
        return acc

    def tile(i, next_scores):
        scores(i, 1)
        acc0 = softmax_pv(0)
        if next_scores:
            scores(i + 1, 0)
        acc1 = softmax_pv(1)
        out_t = jnp.where(low_half, acc0 / acc0[V_HEAD:V_HEAD + 1, :], acc1 / acc1[0:1, :])
        o_ref[tile_rows(i), :] = out_t.T.astype(BF16)

    scores(0, 0)

    def body(i, carry):
        tile(i, True)
        return carry

    lax.fori_loop(0, n_q - 1, body, 0)
    tile(n_q - 1, False)


def _attn(q, kvs, tq, tk):
    bsz, _, lq, _ = q.shape
    n_pairs = MLA_HEADS // 2
    total = sum(k.shape[2] for k, _ in kvs)
    in_specs = [pl.BlockSpec((None, 2, lq, HEAD_LANES), lambda b, h: (b, h, 0, 0))]
    args = [q]
    for k, v in kvs:
        t = k.shape[2]
        in_specs.append(pl.BlockSpec((None, 2, t, HEAD_LANES), lambda b, h: (b, h, 0, 0)))
        in_specs.append(pl.BlockSpec((None, t, 2 * V_HEAD), lambda b, h: (b, 0, h)))
        args += [k, v]
    blocks = 2 * (2 * lq * HEAD_LANES + total * (2 * HEAD_LANES + 2 * V_HEAD) + lq * 2 * V_HEAD) * 2
    vmem = blocks + 2 * tq * (total + LANES) * 4 + 2 * total * 2 * V_HEAD * 2 + (16 << 20)
    return pl.pallas_call(
        functools.partial(_attn_kernel, n_kv=len(kvs), tq=tq, tk=tk),
        grid=(bsz, n_pairs),
        in_specs=in_specs,
        out_specs=pl.BlockSpec((None, lq, 2 * V_HEAD), lambda b, h: (b, 0, h)),
        out_shape=jax.ShapeDtypeStruct((bsz, lq, MLA_HEADS * V_HEAD), BF16),
        scratch_shapes=[pltpu.VMEM((2, total, tq), F32), pltpu.VMEM((2, 8, tq), F32),
                        pltpu.VMEM((2, 2 * V_HEAD, total), BF16)],
        compiler_params=_cparams(("arbitrary", "arbitrary"), vmem),
        name="attn",
    )(*args)


def _resid_ffn(mix, x_ref, mod_ref, wo_ref, gf_ref, w1_ref, w3_ref, w2_ref, fg_ref, o_ref, sub):
    tiles = [slice(r0, r0 + sub) for r0 in range(0, x_ref.shape[0], sub)]
    a = [mix(rows) for rows in tiles]
    x1 = [x_ref[rows, :] + mod_ref[2:3, :] * _dot(ak, wo_ref[...]) for rows, ak in zip(tiles, a)]
    hx = [(_rms(xk, gf_ref[...]) * (1.0 + mod_ref[4:5, :]) + mod_ref[3:4, :]).astype(BF16) for xk in x1]
    t = [(_silu(_dot(hk, w1_ref[...])) * _dot(hk, w3_ref[...])).astype(BF16) for hk in hx]
    for rows, xk, tk_ in zip(tiles, x1, t):
        x2 = xk + mod_ref[5:6, :] * _dot(tk_, w2_ref[...])
        if fg_ref is not None:
            x2 = _rms(x2, fg_ref[...])
        o_ref[rows, :] = x2


def _post_mla_kernel(x_ref, a_ref, mod_ref, wo_ref, gf_ref, w1_ref, w3_ref, w2_ref, o_ref, *, sub):
    _resid_ffn(lambda rows: a_ref[rows, :], x_ref, mod_ref, wo_ref, gf_ref, w1_ref, w3_ref, w2_ref,
               None, o_ref, sub)


def _post_ssd_kernel(x_ref, y_ref, z_ref, ng_ref, mod_ref, wo_ref, gf_ref, w1_ref, w3_ref, w2_ref,
                     fg_ref, o_ref, *, sub):
    def gated_norm(rows):
        gated = y_ref[rows, :].astype(F32) * _silu(z_ref[rows, :].astype(F32))
        return _rms(gated, ng_ref[...]).astype(BF16)

    _resid_ffn(gated_norm, x_ref, mod_ref, wo_ref, gf_ref, w1_ref, w3_ref, w2_ref, fg_ref, o_ref, sub)


def _layer_spec(stacked, layer):
    return pl.BlockSpec((None,) + stacked.shape[1:], lambda *_: (layer, 0, 0),
                        pipeline_mode=pl.Buffered(1))


def _post_weight_specs(w):
    layer = w["layer"]
    return [_const_spec(w["wo"].shape), _const_spec((1, D_MODEL)), _layer_spec(w["w1"], layer),
            _layer_spec(w["w3"], layer), _layer_spec(w["w2"], layer)]


def _post_mla(x, a, mod, w, tm, sub=256):
    bsz, seq, _ = x.shape
    row = lambda n: pl.BlockSpec((None, tm, n), lambda b, i: (b, i, 0))
    return pl.pallas_call(
        functools.partial(_post_mla_kernel, sub=min(sub, tm)),
        grid=(bsz, seq // tm),
        in_specs=[row(D_MODEL), row(a.shape[2]), _mod_spec(mod)] + _post_weight_specs(w),
        out_specs=row(D_MODEL),
        out_shape=jax.ShapeDtypeStruct((bsz, seq, D_MODEL), F32),
        compiler_params=_cparams(("arbitrary", "arbitrary"), VMEM_LIMIT_CAP),
        name="post_mla",
    )(x, a, mod, w["wo"], w["gf"], w["w1"], w["w3"], w["w2"])


def _post_ssd(x, y, z, norm_gain, mod, w, final_gain, tm, sub=256):
    bsz, seq, _ = x.shape
    row = lambda n: pl.BlockSpec((None, tm, n), lambda b, i: (b, i, 0))
    return pl.pallas_call(
        functools.partial(_post_ssd_kernel, sub=min(sub, tm)),
        grid=(bsz, seq // tm),
        in_specs=[row(D_MODEL), row(SSD_INNER), row(SSD_INNER), _const_spec((1, SSD_INNER)),
                  _mod_spec(mod)] + _post_weight_specs(w) + [_const_spec((1, D_MODEL))],
        out_specs=row(D_MODEL),
        out_shape=jax.ShapeDtypeStruct((bsz, seq, D_MODEL), F32),
        compiler_params=_cparams(("arbitrary", "arbitrary"), VMEM_LIMIT_CAP),
        name="post_ssd",
    )(x, y, z, norm_gain, mod, w["wo"], w["gf"], w["w1"], w["w3"], w["w2"], final_gain)


def _ssd_in_kernel(x_ref, mod_ref, g_ref, wz_ref, wx_ref, wdh_ref, wdl_ref, db_ref,
                   z_ref, xbc_ref, dt_ref):
    h = _rms(x_ref[...], g_ref[...]) * (1.0 + mod_ref[1:2, :]) + mod_ref[0:1, :]
    hb = h.astype(BF16)
    z_ref[...] = _dot(hb, wz_ref[...]).astype(BF16)
    xbc_ref[...] = _dot(hb, wx_ref[...]).astype(BF16)
    hl = (h - hb.astype(F32)).astype(BF16)
    raw = _dot(hb, wdh_ref[...]) + _dot(hl, wdh_ref[...]) + _dot(hb, wdl_ref[...])
    raw = raw + db_ref[...]
    dt_ref[...] = jnp.maximum(raw, 0.0) + jnp.log(1.0 + jnp.exp(-jnp.abs(raw)))


def _ssd_in(x, mod, w, tm):
    bsz, seq, _ = x.shape
    row = lambda n: pl.BlockSpec((None, tm, n), lambda b, i: (b, i, 0))
    return pl.pallas_call(
        _ssd_in_kernel,
        grid=(bsz, seq // tm),
        in_specs=[row(D_MODEL), _mod_spec(mod), _const_spec((1, D_MODEL)),
                  _const_spec(w["wz"].shape), _const_spec(w["wx"].shape),
                  _const_spec(w["wdh"].shape), _const_spec(w["wdl"].shape),
                  _const_spec((1, LANES))],
        out_specs=[row(SSD_INNER), row(SSD_CONV_DIM), row(LANES)],
        out_shape=[jax.ShapeDtypeStruct((bsz, seq, SSD_INNER), BF16),
                   jax.ShapeDtypeStruct((bsz, seq, SSD_CONV_DIM), BF16),
                   jax.ShapeDtypeStruct((bsz, seq, LANES), F32)],
        compiler_params=_cparams(("arbitrary", "arbitrary"), 48 << 20),
        name="ssd_in",
    )(x, mod, w["g"], w["wz"], w["wx"], w["wdh"], w["wdl"], w["db"])


CONV_PAD = 16
LOG2E = math.log2(math.e)


def _ssd_kernel(xc_ref, bc_ref, cc_ref, dtc_ref, xx_ref, bx_ref, cx_ref, dtx_ref,
                alog_ref, dsk_ref, cwx_ref, cwb_ref, cwc_ref, cbx_ref, cbb_ref, cbc_ref,
                y_ref, s_ref, cum_ref, cumt_ref, dtt_ref, btr_ref, ux_ref, ub_ref, uc_ref, shift_ref):
    g = pl.program_id(1)
    n_ctx = xc_ref.shape[0] // CHUNK
    n_lat = xx_ref.shape[0] // CHUNK
    a_all = jnp.broadcast_to(-jnp.exp(alog_ref[...]), (8, LANES))
    qi = lax.broadcasted_iota(jnp.int32, (CHUNK, CHUNK), 0)
    ki = lax.broadcasted_iota(jnp.int32, (CHUNK, CHUNK), 1)
    lane = lax.broadcasted_iota(jnp.int32, (1, LANES), 1)
    low_half = lane < SSD_HEADDIM
    n_pair = HEADS_PER_GROUP // 2

    def pair_cols(v, pr):
        return jnp.where(low_half, v[:, 2 * pr:2 * pr + 1], v[:, 2 * pr + 1:2 * pr + 2])

    def split3(v):
        hi = v.astype(BF16)
        r1 = v - hi.astype(F32)
        mid = r1.astype(BF16)
        return hi, mid, (r1 - mid.astype(F32)).astype(BF16)

    def lane_bcast(v, col):
        return jnp.broadcast_to(v[:, col:col + 1], (v.shape[0], LANES))

    sel_r = lax.broadcasted_iota(jnp.int32, (CHUNK, CHUNK + 2 * CONV_PAD), 0)
    sel_c = lax.broadcasted_iota(jnp.int32, (CHUNK, CHUNK + 2 * CONV_PAD), 1)
    shifted_taps = [j for j in range(SSD_CONV) if j != SSD_CONV // 2]
    for n, j in enumerate(shifted_taps):
        hit = sel_c == sel_r + (CONV_PAD - SSD_CONV // 2 + j)
        shift_ref[n * CHUNK:(n + 1) * CHUNK, :] = jnp.where(hit, 1.0, 0.0).astype(BF16)

    def window(raw_ref, r0, first, last):
        pad = jnp.zeros((CONV_PAD, raw_ref.shape[1]), BF16)
        if first and last:
            return jnp.concatenate([pad, raw_ref[0:CHUNK, :], pad], axis=0)
        if first:
            return jnp.concatenate([pad, raw_ref[0:CHUNK + CONV_PAD, :]], axis=0)
        if last:
            return jnp.concatenate([raw_ref[r0 - CONV_PAD:r0 + CHUNK, :], pad], axis=0)
        return raw_ref[pl.ds(pl.multiple_of(r0 - CONV_PAD, CONV_PAD), CHUNK + 2 * CONV_PAD), :]

    def scan_sums(x_ref, b_ref, c_ref, dt_ref, r0, chunk_id, first, last):
        grow = pl.ds(chunk_id * CHUNK if isinstance(chunk_id, int)
                     else pl.multiple_of(chunk_id * CHUNK, CHUNK), CHUNK)
        win = jnp.concatenate([window(r, r0, first, last) for r in (x_ref, b_ref, c_ref)], axis=1)
        taps = _dot(shift_ref[...], win)
        cw = jnp.concatenate([cwx_ref[...], cwb_ref[...], cwc_ref[...]], axis=1)
        acc = jnp.concatenate([cbx_ref[...], cbb_ref[...], cbc_ref[...]], axis=1)
        mid = SSD_CONV // 2
        acc = acc + win[CONV_PAD:CONV_PAD + CHUNK, :].astype(F32) * cw[mid:mid + 1, :]
        for n, j in enumerate(shifted_taps):
            acc = acc + taps[n * CHUNK:(n + 1) * CHUNK, :] * cw[j:j + 1, :]
        u = _silu(acc).astype(BF16)
        ub = u[:, GROUP_W:GROUP_W + SSD_STATE]
        ux_ref[grow, :] = u[:, :GROUP_W]
        ub_ref[grow, :] = ub
        uc_ref[grow, :] = u[:, GROUP_W + SSD_STATE:]
        btr_ref[chunk_id] = ub.astype(F32).T
        dt = dt_ref[pl.ds(r0, CHUNK), :]
        for direction in range(2):
            causal = (ki <= qi) if direction == 0 else (ki >= qi)
            shift = (LANES - (direction * SSD_HEADS + g * HEADS_PER_GROUP)) & (LANES - 1)
            dts = pltpu.roll(dt, shift, 1)
            a = dts * pltpu.roll(a_all, shift, 1)[0:1, :]
            tri = jnp.where(causal, 1.0, 0.0).astype(BF16)
            r = _dot(tri, jnp.concatenate(split3(a), axis=1))
            c = (r[:, :LANES] + r[:, LANES:2 * LANES] + r[:, 2 * LANES:]) * LOG2E
            cum_ref[direction, chunk_id] = c
            cumt_ref[direction, chunk_id] = c.T[0:8, :]
            dtt_ref[direction, chunk_id] = dts.T[0:8, :]

    def scan_step(chunk_id, direction, emit):
        grow = pl.ds(chunk_id * CHUNK if isinstance(chunk_id, int)
                     else pl.multiple_of(chunk_id * CHUNK, CHUNK), CHUNK)
        st_ref = s_ref.at[direction]
        causal = (ki <= qi) if direction == 0 else (ki >= qi)
        c = cum_ref[direction, chunk_id]
        c_t = cumt_ref[direction, chunk_id]
        dt_t = dtt_ref[direction, chunk_id]
        end = CHUNK - 1 if direction == 0 else 0
        to_end_t = jnp.exp2(c_t[:, end:end + 1] - c_t) * dt_t
        cd = jnp.exp2(c[end:end + 1, :])
        b_t = btr_ref[chunk_id]
        x = ux_ref[grow, :]
        if emit:
            cmat = uc_ref[grow, :]
            cb = _dot_nt(cmat, ub_ref[grow, :])
            y_off = _dot(cmat, st_ref[...].astype(BF16))
        ys, upds = [], []
        for pr in range(n_pair):
            heads = (2 * pr, 2 * pr + 1)
            ps = slice(pr * LANES, (pr + 1) * LANES)
            xp = x[:, ps]
            zero = jnp.zeros_like(xp)
            rhs = jnp.concatenate([jnp.where(low_half, xp, zero),
                                   jnp.where(low_half, zero, xp)], axis=0)
            bts = jnp.concatenate([(b_t * to_end_t[hd:hd + 1, :]).astype(BF16) for hd in heads], axis=1)
            upds.append(_dot(bts, rhs))
            if emit:
                ws, e_b = [], []
                for hd in heads:
                    cq = lane_bcast(c, hd)
                    dec = jnp.exp2(jnp.where(causal, cq - c_t[hd:hd + 1, :], -jnp.inf))
                    ws.append((cb * dec * dt_t[hd:hd + 1, :]).astype(BF16))
                    e_b.append(jnp.exp2(cq))
                ys.append(_dot(jnp.concatenate(ws, axis=1), rhs)
                          + jnp.where(low_half, e_b[0], e_b[1]) * y_off[:, ps])
        cds = jnp.concatenate([pair_cols(cd, pr) for pr in range(n_pair)], axis=1)
        st_ref[...] = st_ref[...] * cds + jnp.concatenate(upds, axis=1)
        return (jnp.concatenate(ys, axis=1), x) if emit else (None, None)

    dsk = dsk_ref[...]
    s_ref[...] = jnp.zeros_like(s_ref)
    for ci in range(n_ctx):
        scan_sums(xc_ref, bc_ref, cc_ref, dtc_ref, ci * CHUNK, ci, ci == 0, ci == n_ctx - 1)

    assert n_lat >= 2
    scan_sums(xx_ref, bx_ref, cx_ref, dtx_ref, 0, n_ctx, True, False)

    def sums_body(j, carry):
        scan_sums(xx_ref, bx_ref, cx_ref, dtx_ref, pl.multiple_of(j * CHUNK, CHUNK), n_ctx + j,
                  False, False)
        return carry

    lax.fori_loop(1, n_lat - 1, sums_body, 0, unroll=2)
    scan_sums(xx_ref, bx_ref, cx_ref, dtx_ref, (n_lat - 1) * CHUNK, n_ctx + n_lat - 1, False, True)

    for ci in range(n_ctx):
        for direction, cj in ((0, ci), (1, n_ctx - 1 - ci)):
            scan_step(cj, direction, False)

    def body(i, carry, first_touch):
        for direction in range(2):
            chunk = i if direction == 0 else n_lat - 1 - i
            rows = pl.ds(pl.multiple_of(chunk * CHUNK, CHUNK), CHUNK)
            y, x = scan_step(n_ctx + chunk, direction, True)
            if direction == 0:
                y = y + dsk * x.astype(F32)
            if not first_touch:
                y = y + y_ref[rows, :].astype(F32)
            y_ref[rows, :] = y.astype(BF16)
        return carry

    assert n_lat % 2 == 0
    lax.fori_loop(0, n_lat // 2, functools.partial(body, first_touch=True), 0, unroll=2)
    lax.fori_loop(n_lat // 2, n_lat, functools.partial(body, first_touch=False), 0, unroll=2)


def _ssd(uc, dtc, ux, dtx, alog, dsk, conv_w, conv_b):
    bsz, lc, _ = uc.shape
    lx = ux.shape[1]
    n_chunks = (lc + lx) // CHUNK
    b_blk0 = SSD_INNER // SSD_STATE
    c_blk0 = b_blk0 + SSD_GROUPS
    chan_specs = [(GROUP_W, lambda b, g: (0, g)),
                  (SSD_STATE, lambda b, g: (0, b_blk0 + g)),
                  (SSD_STATE, lambda b, g: (0, c_blk0 + g))]

    def seg_specs(rows):
        return [pl.BlockSpec((None, rows, GROUP_W), lambda b, g: (b, 0, g)),
                pl.BlockSpec((None, rows, SSD_STATE), lambda b, g: (b, 0, b_blk0 + g)),
                pl.BlockSpec((None, rows, SSD_STATE), lambda b, g: (b, 0, c_blk0 + g)),
                pl.BlockSpec((None, rows, LANES), lambda b, g: (b, 0, 0))]

    return pl.pallas_call(
        _ssd_kernel,
        grid=(bsz, SSD_GROUPS),
        in_specs=seg_specs(lc) + seg_specs(lx) + [
            pl.BlockSpec((1, LANES), lambda b, g: (0, 0)),
            pl.BlockSpec((1, GROUP_W), lambda b, g: (0, g))]
        + [pl.BlockSpec((SSD_CONV, wd), f) for wd, f in chan_specs]
        + [pl.BlockSpec((1, wd), f) for wd, f in chan_specs],
        out_specs=pl.BlockSpec((None, lx, GROUP_W), lambda b, g: (b, 0, g)),
        out_shape=jax.ShapeDtypeStruct((bsz, lx, SSD_INNER), BF16),
        scratch_shapes=[
            pltpu.VMEM((2, SSD_STATE, GROUP_W), F32),
            pltpu.VMEM((2, n_chunks, CHUNK, LANES), F32),
            pltpu.VMEM((2, n_chunks, 8, CHUNK), F32),
            pltpu.VMEM((2, n_chunks, 8, CHUNK), F32),
            pltpu.VMEM((n_chunks, SSD_STATE, CHUNK), F32),
            pltpu.VMEM((lc + lx, GROUP_W), BF16),
            pltpu.VMEM((lc + lx, SSD_STATE), BF16),
            pltpu.VMEM((lc + lx, SSD_STATE), BF16),
            pltpu.VMEM(((SSD_CONV - 1) * CHUNK, CHUNK + 2 * CONV_PAD), BF16),
        ],
        compiler_params=_cparams(("arbitrary", "arbitrary"), 54 << 20),
        name="ssd",
    )(uc, uc, uc, dtc, ux, ux, ux, dtx, alog, dsk, conv_w, conv_w, conv_w, conv_b, conv_b, conv_b)


def _rope_partner_index():
    j = np.arange(QK_ROPE)
    return np.where(j % 16 < 8, j + 8, j - 8)


def _rope_tables(seq):
    t = np.arange(seq)
    pos = np.stack([t // GRID_W, t % GRID_W], axis=1).astype(np.float64)
    j = np.arange(QK_ROPE)
    inv = ROPE_THETA ** (-np.arange(8, dtype=np.float64) * 2.0 / 16.0)
    ang = pos[:, j // 16] * inv[j % 8][None, :]
    sign = np.where(j % 16 < 8, -1.0, 1.0)
    return np.cos(ang), np.sin(ang) * sign


def _mla_tables(seq, rotate):
    scale = math.log2(math.e) / math.sqrt(QK_NOPE + QK_ROPE)
    if rotate:
        cos, sin = _rope_tables(seq)
    else:
        cos, sin = np.ones((seq, QK_ROPE)), np.zeros((seq, QK_ROPE))
    one = np.ones((seq, QK_NOPE))
    zero = np.zeros((seq, QK_NOPE))
    tabs = {"q": np.concatenate([one, cos, sin], axis=1) * scale,
            "kc": np.concatenate([zero, cos, cos], axis=1),
            "ks": np.concatenate([zero, sin, sin], axis=1)}
    return {k: jnp.asarray(v.astype(np.float32)) for k, v in tabs.items()}


def _mla_weights(w_in, q_norm, w_uq, kv_norm, w_ukv):
    partner = _rope_partner_index()
    kr = w_in[:, Q_LORA + KV_LORA:]
    krp = kr[:, partner]
    zero = jnp.zeros((D_MODEL, QK_NOPE), F32)
    win = jnp.concatenate([w_in[:, :Q_LORA + KV_LORA], zero, kr, kr, zero, krp, krp], axis=1)
    uq = w_uq.reshape(Q_LORA, MLA_HEADS, QK_NOPE + QK_ROPE)
    rope = uq[:, :, QK_NOPE:]
    wq = jnp.concatenate([uq, rope[:, :, partner]], axis=2).reshape(Q_LORA, MLA_HEADS * HEAD_LANES)
    ukv = w_ukv.reshape(KV_LORA, MLA_HEADS, QK_NOPE + V_HEAD)
    wk = jnp.concatenate([ukv[:, :, :QK_NOPE], jnp.zeros((KV_LORA, MLA_HEADS, QK_NOPE), F32)], axis=2)
    return {"win": win.astype(BF16), "qg": q_norm.reshape(1, -1), "wq": wq.astype(BF16),
            "kvg": kv_norm.reshape(1, -1), "wk": wk.reshape(KV_LORA, -1).astype(BF16),
            "wv": ukv[:, :, QK_NOPE:].reshape(KV_LORA, -1).astype(BF16)}


def _post_weights(wo, gf, ffn, layer):
    return {"wo": wo.astype(BF16), "gf": gf.reshape(1, -1), "layer": layer, **ffn}


def _ssd_in_weights(gain, w_in, dt_bias):
    wd = jnp.pad(w_in[:, SSD_INNER + SSD_CONV_DIM:], ((0, 0), (0, LANES - 2 * SSD_HEADS)))
    wdh = wd.astype(BF16)
    wdl = (wd - wdh.astype(F32)).astype(BF16)
    db = jnp.pad(dt_bias.reshape(1, -1), ((0, 0), (0, LANES - 2 * SSD_HEADS)))
    return {"g": gain.reshape(1, -1), "wz": w_in[:, :SSD_INNER].astype(BF16),
            "wx": w_in[:, SSD_INNER:SSD_INNER + SSD_CONV_DIM].astype(BF16),
            "wdh": wdh, "wdl": wdl, "db": db}


def kernel(x, c, ctx, c_ctx, ada_w, ada_b, norm_mix, norm_ffn, ffn_w1, ffn_w3, ffn_w2,
           mla_w_in, mla_q_norm, mla_w_uq, mla_kv_norm, mla_w_ukv, mla_w_o,
           ssd_w_in, ssd_conv_w, ssd_conv_b, ssd_dt_bias, ssd_a_log, ssd_d, ssd_norm, ssd_w_o,
           final_norm):
    bsz, seq, _ = x.shape
    lc = ctx.shape[1]
    tm = 256

    cc = jnp.zeros((8, D_MODEL), F32).at[:bsz].set(c).at[bsz].set(c_ctx)
    mod = _ada(cc, ada_w, ada_b)
    mod_x = [mod[l, :bsz].reshape(bsz, 6, D_MODEL) for l in range(DEPTH)]
    mod_c = [mod[l, bsz:bsz + 1].reshape(1, 6, D_MODEL) for l in range(DEPTH)]

    wm = _mla_weights(mla_w_in[0], mla_q_norm[0], mla_w_uq[0], mla_kv_norm[0], mla_w_ukv[0])
    g0 = norm_mix[0].reshape(1, -1)
    q_c, k_c, v_c = _mla_proj(ctx, mod_c[0], g0, wm, _mla_tables(lc, False), tm)
    q_x, k_x, v_x = _mla_proj(x, mod_x[0], g0, wm, _mla_tables(seq, True), tm)
    o_x = _attn(q_x, [(k_c, v_c), (k_x, v_x)], tq=256, tk=512)
    o_c = _attn(q_c, [(k_c, v_c)], tq=lc, tk=512)
    ffn = {"w1": ffn_w1.astype(BF16), "w3": ffn_w3.astype(BF16), "w2": ffn_w2.astype(BF16)}
    wp0 = _post_weights(mla_w_o[0], norm_ffn[0], ffn, 0)
    x = _post_mla(x, o_x, mod_x[0], wp0, 2 * tm)
    ctx = _post_mla(ctx, o_c, mod_c[0], wp0, tm)

    ws = _ssd_in_weights(norm_mix[1], ssd_w_in[0], ssd_dt_bias[0])
    _, xbc_c, dt_c = _ssd_in(ctx, mod_c[1], ws, tm)
    z_x, xbc_x, dt_x = _ssd_in(x, mod_x[1], ws, tm)
    alog = jnp.pad(ssd_a_log[0].reshape(1, -1), ((0, 0), (0, LANES - 2 * SSD_HEADS)))
    dsk = jnp.repeat(ssd_d[0], SSD_HEADDIM).reshape(1, -1)
    y = _ssd(xbc_c, dt_c, xbc_x, dt_x, alog, dsk, ssd_conv_w[0], ssd_conv_b[0].reshape(1, -1))
    wp1 = _post_weights(ssd_w_o[0], norm_ffn[1], ffn, 1)
    return _post_ssd(x, y, z_x, ssd_norm[0].reshape(1, -1), mod_x[1], wp1,
                     final_norm.reshape(1, -1), 2 * tm)
```

```python
import functools
import math

import numpy as np
import jax
import jax.numpy as jnp
from jax import lax
from jax.experimental import pallas as pl
from jax.experimental.pallas import tpu as pltpu

D_MODEL = 1024
DEPTH = 2
GRID_W = 64
EPS = 1e-6
MLA_HEADS = 16
Q_LORA = 512
KV_LORA = 256
QK_NOPE = 64
QK_ROPE = 32
V_HEAD = 64
ROPE_THETA = 10000.0
SSD_INNER = 2 * D_MODEL
SSD_HEADDIM = 64
SSD_HEADS = SSD_INNER // SSD_HEADDIM
SSD_GROUPS = 4
SSD_STATE = 128
SSD_CONV = 5
SSD_CONV_DIM = SSD_INNER + 2 * SSD_GROUPS * SSD_STATE
CHUNK = 128
FFN_HIDDEN = ((8 * D_MODEL // 3 + 255) // 256) * 256

LANES = 128
V7X_VMEM_BYTES = 64 * 1024 * 1024
VMEM_LIMIT_CAP = 56 * 1024 * 1024

F32 = jnp.float32
BF16 = jnp.bfloat16
HEAD_LANES = 2 * QK_NOPE
HEADS_PER_GROUP = SSD_HEADS // SSD_GROUPS
GROUP_W = HEADS_PER_GROUP * SSD_HEADDIM


def _cparams(semantics, vmem_bytes):
    return pltpu.CompilerParams(
        dimension_semantics=semantics,
        vmem_limit_bytes=int(min(VMEM_LIMIT_CAP, vmem_bytes)),
    )


def _const_spec(shape):
    nd = len(shape)
    return pl.BlockSpec(shape, lambda *_: (0,) * nd, pipeline_mode=pl.Buffered(1))


def _rms(x, g):
    return x * lax.rsqrt(jnp.mean(x * x, axis=-1, keepdims=True) + EPS) * g


def _silu(x):
    return x * (1.0 / (1.0 + jnp.exp(-x)))


def _dot(a, b):
    return jnp.dot(a, b, preferred_element_type=F32)


def _dot_nt(a, b):
    return lax.dot_general(a, b, (((1,), (1,)), ((), ())), preferred_element_type=F32)


def _dot_f32(a, b):
    return jnp.dot(a, b, preferred_element_type=F32, precision=lax.Precision.HIGHEST)


def _ada_kernel(c_ref, w_ref, b_ref, o_ref):
    s = _silu(c_ref[...])
    o_ref[...] = _dot_f32(s, w_ref[...]) + b_ref[...]


def _ada(cc, ada_w, ada_b):
    n_tiles = ada_w.shape[2] // D_MODEL
    return pl.pallas_call(
        _ada_kernel,
        grid=(DEPTH, n_tiles),
        in_specs=[
            pl.BlockSpec((8, D_MODEL), lambda l, j: (0, 0)),
            pl.BlockSpec((None, D_MODEL, D_MODEL), lambda l, j: (l, 0, j)),
            pl.BlockSpec((None, 1, D_MODEL), lambda l, j: (l, 0, j)),
        ],
        out_specs=pl.BlockSpec((None, 8, D_MODEL), lambda l, j: (l, 0, j)),
        out_shape=jax.ShapeDtypeStruct((DEPTH, 8, ada_w.shape[2]), F32),
        compiler_params=_cparams(("arbitrary", "arbitrary"), 24 << 20),
        name="ada",
    )(cc, ada_w, ada_b.reshape(DEPTH, 1, -1))


def _mod_spec(mod):
    if mod.shape[0] == 1:
        return pl.BlockSpec((None, 6, D_MODEL), lambda b, i: (0, 0, 0))
    return pl.BlockSpec((None, 6, D_MODEL), lambda b, i: (b, 0, 0))


def _mla_proj_kernel(x_ref, mod_ref, g_ref, win_ref, qg_ref, wq_ref, kvg_ref, wk_ref, wv_ref,
                     tq_ref, tkc_ref, tks_ref, q_ref, k_ref, v_ref, *, sub):
    tiles = [slice(r0, r0 + sub) for r0 in range(0, x_ref.shape[0], sub)]
    hs = [(_rms(x_ref[rows, :], g_ref[...]) * (1.0 + mod_ref[1:2, :]) + mod_ref[0:1, :]).astype(BF16)
          for rows in tiles]
    ps = [_dot(h, win_ref[...]) for h in hs]
    cqns = [_rms(p[:, :Q_LORA], qg_ref[...]).astype(BF16) for p in ps]
    ckvns = [_rms(p[:, Q_LORA:Q_LORA + KV_LORA], kvg_ref[...]).astype(BF16) for p in ps]
    for rows, p, cqn, ckvn in zip(tiles, ps, cqns, ckvns):
        kr_a = p[:, Q_LORA + KV_LORA:Q_LORA + KV_LORA + LANES]
        kr_b = p[:, Q_LORA + KV_LORA + LANES:]
        qf = _dot(cqn, wq_ref[...])
        kf = _dot(ckvn, wk_ref[...])
        kr = kr_a * tkc_ref[rows, :] + kr_b * tks_ref[rows, :]
        tq = tq_ref[rows, :]
        for hd in range(MLA_HEADS):
            sl = slice(hd * HEAD_LANES, (hd + 1) * HEAD_LANES)
            q_ref[hd, rows, :] = (qf[:, sl] * tq).astype(BF16)
            k_ref[hd, rows, :] = (kf[:, sl] + kr).astype(BF16)
        v_ref[rows, :] = _dot(ckvn, wv_ref[...]).astype(BF16)


def _mla_proj(x, mod, gain, w, tabs, tm, sub=256):
    bsz, seq, _ = x.shape
    row = lambda n: pl.BlockSpec((None, tm, n), lambda b, i: (b, i, 0))
    heads = pl.BlockSpec((None, MLA_HEADS, tm, HEAD_LANES), lambda b, i: (b, 0, i, 0))
    tab = pl.BlockSpec((tm, LANES), lambda b, i: (i, 0))
    return pl.pallas_call(
        functools.partial(_mla_proj_kernel, sub=min(sub, tm)),
        grid=(bsz, seq // tm),
        in_specs=[
            row(D_MODEL), _mod_spec(mod), _const_spec((1, D_MODEL)),
            _const_spec(w["win"].shape), _const_spec((1, Q_LORA)), _const_spec(w["wq"].shape),
            _const_spec((1, KV_LORA)), _const_spec(w["wk"].shape), _const_spec(w["wv"].shape),
            tab, tab, tab,
        ],
        out_specs=[heads, heads, row(MLA_HEADS * V_HEAD)],
        out_shape=[
            jax.ShapeDtypeStruct((bsz, MLA_HEADS, seq, HEAD_LANES), BF16),
            jax.ShapeDtypeStruct((bsz, MLA_HEADS, seq, HEAD_LANES), BF16),
            jax.ShapeDtypeStruct((bsz, seq, MLA_HEADS * V_HEAD), BF16),
        ],
        compiler_params=_cparams(("arbitrary", "arbitrary"), 40 << 20),
        name="mla_proj",
    )(x, mod, gain, w["win"], w["qg"], w["wq"], w["kvg"], w["wk"], w["wv"],
      tabs["q"], tabs["kc"], tabs["ks"])


def _attn_kernel(*refs, n_kv, tq, tk):
    q_ref = refs[0]
    kv_refs = [(refs[1 + 2 * j], refs[2 + 2 * j]) for j in range(n_kv)]
    o_ref = refs[1 + 2 * n_kv]
    s_ref, m_ref, vm_ref = refs[2 + 2 * n_kv:]
    n_q = q_ref.shape[1] // tq
    chan = lax.broadcasted_iota(jnp.int32, (2 * V_HEAD, 1), 0)
    low_half = chan < V_HEAD
    chunks = []
    col = 0
    for k_ref, v_ref in kv_refs:
        rows = k_ref.shape[1]
        v_t = v_ref[...].astype(F32).T
        vm_ref[0, :, col:col + rows] = jnp.where(low_half, v_t, 1.0).astype(BF16)
        vm_ref[1, :, col:col + rows] = jnp.where(low_half, 1.0, v_t).astype(BF16)
        step = min(tk, rows)
        for r0 in range(0, rows, step):
            chunks.append((k_ref, r0, step, col + r0))
        col += rows

    def tile_rows(i):
        start = i * tq
        return pl.ds(start if isinstance(start, int) else pl.multiple_of(start, tq), tq)

    def scores(i, hh):
        qh = q_ref[hh, tile_rows(i), :]
        m = None
        for k_ref, r0, rows, c0 in chunks:
            s_t = _dot_nt(k_ref[hh, r0:r0 + rows, :], qh)
            s_ref[hh, c0:c0 + rows, :] = s_t
            cm = jnp.max(s_t, axis=0, keepdims=True)
            m = cm if m is None else jnp.maximum(m, cm)
        m_ref[hh] = jnp.broadcast_to(m, (8, tq))

    def softmax_pv(hh):
        acc = jnp.zeros((2 * V_HEAD, tq), F32)
        for _, _, rows, c0 in chunks:
            p_t = jnp.exp2(s_ref[hh, c0:c0 + rows, :] - m_ref[hh, 0:1, :]).astype(BF16)
            acc = acc + _dot(vm_ref[hh, :, c0:c0 + rows], p_t)
        return acc

    def tile(i, next_scores):
        scores(i, 1)
        acc0 = softmax_pv(0)
        if next_scores:
            scores(i + 1, 0)
        acc1 = softmax_pv(1)
        out_t = jnp.where(low_half, acc0 / acc0[V_HEAD:V_HEAD + 1, :], acc1 / acc1[0:1, :])
        o_ref[tile_rows(i), :] = out_t.T.astype(BF16)

    scores(0, 0)

    def body(i, carry):
        tile(i, True)
        return carry

    lax.fori_loop(0, n_q - 1, body, 0)
    tile(n_q - 1, False)


def _attn(q, kvs, tq, tk):
    bsz, _, lq, _ = q.shape
    n_pairs = MLA_HEADS // 2
    total = sum(k.shape[2] for k, _ in kvs)
    in_specs = [pl.BlockSpec((None, 2, lq, HEAD_LANES), lambda b, h: (b, h, 0, 0))]
    args = [q]
    for k, v in kvs:
        t = k.shape[2]
        in_specs.append(pl.BlockSpec((None, 2, t, HEAD_LANES), lambda b, h: (b, h, 0, 0)))
        in_specs.append(pl.BlockSpec((None, t, 2 * V_HEAD), lambda b, h: (b, 0, h)))
        args += [k, v]
    blocks = 2 * (2 * lq * HEAD_LANES + total * (2 * HEAD_LANES + 2 * V_HEAD) + lq * 2 * V_HEAD) * 2
    vmem = blocks + 2 * tq * (total + LANES) * 4 + 2 * total * 2 * V_HEAD * 2 + (16 << 20)
    return pl.pallas_call(
        functools.partial(_attn_kernel, n_kv=len(kvs), tq=tq, tk=tk),
        grid=(bsz, n_pairs),
        in_specs=in_specs,
        out_specs=pl.BlockSpec((None, lq, 2 * V_HEAD), lambda b, h: (b, 0, h)),
        out_shape=jax.ShapeDtypeStruct((bsz, lq, MLA_HEADS * V_HEAD), BF16),
        scratch_shapes=[pltpu.VMEM((2, total, tq), F32), pltpu.VMEM((2, 8, tq), F32),
                        pltpu.VMEM((2, 2 * V_HEAD, total), BF16)],
        compiler_params=_cparams(("arbitrary", "arbitrary"), vmem),
        name="attn",
    )(*args)


def _resid_ffn(mix, x_ref, mod_ref, wo_ref, gf_ref, w1_ref, w3_ref, w2_ref, fg_ref, o_ref, sub):
    tiles = [slice(r0, r0 + sub) for r0 in range(0, x_ref.shape[0], sub)]
    a = [mix(rows) for rows in tiles]
    x1 = [x_ref[rows, :] + mod_ref[2:3, :] * _dot(ak, wo_ref[...]) for rows, ak in zip(tiles, a)]
    hx = [(_rms(xk, gf_ref[...]) * (1.0 + mod_ref[4:5, :]) + mod_ref[3:4, :]).astype(BF16) for xk in x1]
    t = [(_silu(_dot(hk, w1_ref[...])) * _dot(hk, w3_ref[...])).astype(BF16) for hk in hx]
    for rows, xk, tk_ in zip(tiles, x1, t):
        x2 = xk + mod_ref[5:6, :] * _dot(tk_, w2_ref[...])
        if fg_ref is not None:
            x2 = _rms(x2, fg_ref[...])
        o_ref[rows, :] = x2


def _post_mla_kernel(x_ref, a_ref, mod_ref, wo_ref, gf_ref, w1_ref, w3_ref, w2_ref, o_ref, *, sub):
    _resid_ffn(lambda rows: a_ref[rows, :], x_ref, mod_ref, wo_ref, gf_ref, w1_ref, w3_ref, w2_ref,
               None, o_ref, sub)


def _post_ssd_kernel(x_ref, y_ref, z_ref, ng_ref, mod_ref, wo_ref, gf_ref, w1_ref, w3_ref, w2_ref,
                     fg_ref, o_ref, *, sub):
    def gated_norm(rows):
        gated = y_ref[rows, :].astype(F32) * _silu(z_ref[rows, :].astype(F32))
        return _rms(gated, ng_ref[...]).astype(BF16)

    _resid_ffn(gated_norm, x_ref, mod_ref, wo_ref, gf_ref, w1_ref, w3_ref, w2_ref, fg_ref, o_ref, sub)


def _layer_spec(stacked, layer):
    return pl.BlockSpec((None,) + stacked.shape[1:], lambda *_: (layer, 0, 0),
                        pipeline_mode=pl.Buffered(1))


def _post_weight_specs(w):
    layer = w["layer"]
    return [_const_spec(w["wo"].shape), _const_spec((1, D_MODEL)), _layer_spec(w["w1"], layer),
            _layer_spec(w["w3"], layer), _layer_spec(w["w2"], layer)]


def _post_mla(x, a, mod, w, tm, sub=256):
    bsz, seq, _ = x.shape
    row = lambda n: pl.BlockSpec((None, tm, n), lambda b, i: (b, i, 0))
    return pl.pallas_call(
        functools.partial(_post_mla_kernel, sub=min(sub, tm)),
        grid=(bsz, seq // tm),
        in_specs=[row(D_MODEL), row(a.shape[2]), _mod_spec(mod)] + _post_weight_specs(w),
        out_specs=row(D_MODEL),
        out_shape=jax.ShapeDtypeStruct((bsz, seq, D_MODEL), F32),
        compiler_params=_cparams(("arbitrary", "arbitrary"), VMEM_LIMIT_CAP),
        name="post_mla",
    )(x, a, mod, w["wo"], w["gf"], w["w1"], w["w3"], w["w2"])


def _post_ssd(x, y, z, norm_gain, mod, w, final_gain, tm, sub=256):
    bsz, seq, _ = x.shape
    row = lambda n: pl.BlockSpec((None, tm, n), lambda b, i: (b, i, 0))
    return pl.pallas_call(
        functools.partial(_post_ssd_kernel, sub=min(sub, tm)),
        grid=(bsz, seq // tm),
        in_specs=[row(D_MODEL), row(SSD_INNER), row(SSD_INNER), _const_spec((1, SSD_INNER)),
                  _mod_spec(mod)] + _post_weight_specs(w) + [_const_spec((1, D_MODEL))],
        out_specs=row(D_MODEL),
        out_shape=jax.ShapeDtypeStruct((bsz, seq, D_MODEL), F32),
        compiler_params=_cparams(("arbitrary", "arbitrary"), VMEM_LIMIT_CAP),
        name="post_ssd",
    )(x, y, z, norm_gain, mod, w["wo"], w["gf"], w["w1"], w["w3"], w["w2"], final_gain)


def _ssd_in_kernel(x_ref, mod_ref, g_ref, wz_ref, wx_ref, wdh_ref, wdl_ref, db_ref,
                   z_ref, xbc_ref, dt_ref, *, sub):
    tiles = [slice(r0, r0 + sub) for r0 in range(0, x_ref.shape[0], sub)]
    hs = [_rms(x_ref[rows, :], g_ref[...]) * (1.0 + mod_ref[1:2, :]) + mod_ref[0:1, :] for rows in tiles]
    hbs = [h.astype(BF16) for h in hs]
    for rows, hb in zip(tiles, hbs):
        z_ref[rows, :] = _dot(hb, wz_ref[...]).astype(BF16)
    for rows, hb in zip(tiles, hbs):
        xbc_ref[rows, :] = _dot(hb, wx_ref[...]).astype(BF16)
    for rows, h, hb in zip(tiles, hs, hbs):
        hl = (h - hb.astype(F32)).astype(BF16)
        raw = _dot(hb, wdh_ref[...]) + _dot(hl, wdh_ref[...]) + _dot(hb, wdl_ref[...])
        raw = raw + db_ref[...]
        dt_ref[rows, :] = jnp.maximum(raw, 0.0) + jnp.log(1.0 + jnp.exp(-jnp.abs(raw)))


def _ssd_in(x, mod, w, tm, sub=256):
    bsz, seq, _ = x.shape
    row = lambda n: pl.BlockSpec((None, tm, n), lambda b, i: (b, i, 0))
    return pl.pallas_call(
        functools.partial(_ssd_in_kernel, sub=min(sub, tm)),
        grid=(bsz, seq // tm),
        in_specs=[row(D_MODEL), _mod_spec(mod), _const_spec((1, D_MODEL)),
                  _const_spec(w["wz"].shape), _const_spec(w["wx"].shape),
                  _const_spec(w["wdh"].shape), _const_spec(w["wdl"].shape),
                  _const_spec((1, LANES))],
        out_specs=[row(SSD_INNER), row(SSD_CONV_DIM), row(LANES)],
        out_shape=[jax.ShapeDtypeStruct((bsz, seq, SSD_INNER), BF16),
                   jax.ShapeDtypeStruct((bsz, seq, SSD_CONV_DIM), BF16),
                   jax.ShapeDtypeStruct((bsz, seq, LANES), F32)],
        compiler_params=_cparams(("arbitrary", "arbitrary"), 48 << 20),
        name="ssd_in",
    )(x, mod, w["g"], w["wz"], w["wx"], w["wdh"], w["wdl"], w["db"])


CONV_PAD = 16
LOG2E = math.log2(math.e)


def _ssd_kernel(xc_ref, bc_ref, cc_ref, dtc_ref, xx_ref, bx_ref, cx_ref, dtx_ref,
                alog_ref, dsk_ref, cwx_ref, cwb_ref, cwc_ref, cbx_ref, cbb_ref, cbc_ref,
                y_ref, s_ref, cum_ref, cumt_ref, dtt_ref, btr_ref, ux_ref, ub_ref, uc_ref, shift_ref):
    g = pl.program_id(1)
    n_ctx = xc_ref.shape[0] // CHUNK
    n_lat = xx_ref.shape[0] // CHUNK
    a_all = jnp.broadcast_to(-jnp.exp(alog_ref[...]), (8, LANES))
    qi = lax.broadcasted_iota(jnp.int32, (CHUNK, CHUNK), 0)
    ki = lax.broadcasted_iota(jnp.int32, (CHUNK, CHUNK), 1)
    lane = lax.broadcasted_iota(jnp.int32, (1, LANES), 1)
    low_half = lane < SSD_HEADDIM
    n_pair = HEADS_PER_GROUP // 2

    def pair_cols(v, pr):
        return jnp.where(low_half, v[:, 2 * pr:2 * pr + 1], v[:, 2 * pr + 1:2 * pr + 2])

    def split3(v):
        hi = v.astype(BF16)
        r1 = v - hi.astype(F32)
        mid = r1.astype(BF16)
        return hi, mid, (r1 - mid.astype(F32)).astype(BF16)

    def lane_bcast(v, col):
        return jnp.broadcast_to(v[:, col:col + 1], (v.shape[0], LANES))

    sel_r = lax.broadcasted_iota(jnp.int32, (CHUNK, CHUNK + 2 * CONV_PAD), 0)
    sel_c = lax.broadcasted_iota(jnp.int32, (CHUNK, CHUNK + 2 * CONV_PAD), 1)
    shifted_taps = [j for j in range(SSD_CONV) if j != SSD_CONV // 2]
    for n, j in enumerate(shifted_taps):
        hit = sel_c == sel_r + (CONV_PAD - SSD_CONV // 2 + j)
        shift_ref[n * CHUNK:(n + 1) * CHUNK, :] = jnp.where(hit, 1.0, 0.0).astype(BF16)

    def window(raw_ref, r0, first, last):
        pad = jnp.zeros((CONV_PAD, raw_ref.shape[1]), BF16)
        if first and last:
            return jnp.concatenate([pad, raw_ref[0:CHUNK, :], pad], axis=0)
        if first:
            return jnp.concatenate([pad, raw_ref[0:CHUNK + CONV_PAD, :]], axis=0)
        if last:
            return jnp.concatenate([raw_ref[r0 - CONV_PAD:r0 + CHUNK, :], pad], axis=0)
        return raw_ref[pl.ds(pl.multiple_of(r0 - CONV_PAD, CONV_PAD), CHUNK + 2 * CONV_PAD), :]

    def scan_sums(x_ref, b_ref, c_ref, dt_ref, r0, chunk_id, first, last):
        grow = pl.ds(chunk_id * CHUNK if isinstance(chunk_id, int)
                     else pl.multiple_of(chunk_id * CHUNK, CHUNK), CHUNK)
        win = jnp.concatenate([window(r, r0, first, last) for r in (x_ref, b_ref, c_ref)], axis=1)
        taps = _dot(shift_ref[...], win)
        cw = jnp.concatenate([cwx_ref[...], cwb_ref[...], cwc_ref[...]], axis=1)
        acc = jnp.concatenate([cbx_ref[...], cbb_ref[...], cbc_ref[...]], axis=1)
        mid = SSD_CONV // 2
        acc = acc + win[CONV_PAD:CONV_PAD + CHUNK, :].astype(F32) * cw[mid:mid + 1, :]
        for n, j in enumerate(shifted_taps):
            acc = acc + taps[n * CHUNK:(n + 1) * CHUNK, :] * cw[j:j + 1, :]
        u = _silu(acc).astype(BF16)
        ub = u[:, GROUP_W:GROUP_W + SSD_STATE]
        ux_ref[grow, :] = u[:, :GROUP_W]
        ub_ref[grow, :] = ub
        uc_ref[grow, :] = u[:, GROUP_W + SSD_STATE:]
        btr_ref[chunk_id] = ub.astype(F32).T
        dt = dt_ref[pl.ds(r0, CHUNK), :]
        for direction in range(2):
            causal = (ki <= qi) if direction == 0 else (ki >= qi)
            shift = (LANES - (direction * SSD_HEADS + g * HEADS_PER_GROUP)) & (LANES - 1)
            dts = pltpu.roll(dt, shift, 1)
            a = dts * pltpu.roll(a_all, shift, 1)[0:1, :]
            tri = jnp.where(causal, 1.0, 0.0).astype(BF16)
            r = _dot(tri, jnp.concatenate(split3(a), axis=1))
            c = (r[:, :LANES] + r[:, LANES:2 * LANES] + r[:, 2 * LANES:]) * LOG2E
            cum_ref[direction, chunk_id] = c
            cumt_ref[direction, chunk_id] = c.T[0:8, :]
            dtt_ref[direction, chunk_id] = dts.T[0:8, :]

    def scan_step(chunk_id, direction, emit):
        grow = pl.ds(chunk_id * CHUNK if isinstance(chunk_id, int)
                     else pl.multiple_of(chunk_id * CHUNK, CHUNK), CHUNK)
        st_ref = s_ref.at[direction]
        causal = (ki <= qi) if direction == 0 else (ki >= qi)
        c = cum_ref[direction, chunk_id]
        c_t = cumt_ref[direction, chunk_id]
        dt_t = dtt_ref[direction, chunk_id]
        end = CHUNK - 1 if direction == 0 else 0
        to_end_t = jnp.exp2(c_t[:, end:end + 1] - c_t) * dt_t
        cd = jnp.exp2(c[end:end + 1, :])
        b_t = btr_ref[chunk_id]
        x = ux_ref[grow, :]
        if emit:
            cmat = uc_ref[grow, :]
            cb = _dot_nt(cmat, ub_ref[grow, :])
            y_off = _dot(cmat, st_ref[...].astype(BF16))
        ys, upds = [], []
        for pr in range(n_pair):
            heads = (2 * pr, 2 * pr + 1)
            ps = slice(pr * LANES, (pr + 1) * LANES)
            xp = x[:, ps]
            zero = jnp.zeros_like(xp)
            rhs = jnp.concatenate([jnp.where(low_half, xp, zero),
                                   jnp.where(low_half, zero, xp)], axis=0)
            bts = jnp.concatenate([(b_t * to_end_t[hd:hd + 1, :]).astype(BF16) for hd in heads], axis=1)
            upds.append(_dot(bts, rhs))
            if emit:
                ws, e_b = [], []
                for hd in heads:
                    cq = lane_bcast(c, hd)
                    dec = jnp.exp2(jnp.where(causal, cq - c_t[hd:hd + 1, :], -jnp.inf))
                    ws.append((cb * dec * dt_t[hd:hd + 1, :]).astype(BF16))
                    e_b.append(jnp.exp2(cq))
                ys.append(_dot(jnp.concatenate(ws, axis=1), rhs)
                          + jnp.where(low_half, e_b[0], e_b[1]) * y_off[:, ps])
        cds = jnp.concatenate([pair_cols(cd, pr) for pr in range(n_pair)], axis=1)
        st_ref[...] = st_ref[...] * cds + jnp.concatenate(upds, axis=1)
        return (jnp.concatenate(ys, axis=1), x) if emit else (None, None)

    dsk = dsk_ref[...]
    s_ref[...] = jnp.zeros_like(s_ref)
    for ci in range(n_ctx):
        scan_sums(xc_ref, bc_ref, cc_ref, dtc_ref, ci * CHUNK, ci, ci == 0, ci == n_ctx - 1)

    assert n_lat >= 2
    scan_sums(xx_ref, bx_ref, cx_ref, dtx_ref, 0, n_ctx, True, False)

    def sums_body(j, carry):
        scan_sums(xx_ref, bx_ref, cx_ref, dtx_ref, pl.multiple_of(j * CHUNK, CHUNK), n_ctx + j,
                  False, False)
        return carry

    lax.fori_loop(1, n_lat - 1, sums_body, 0, unroll=2)
    scan_sums(xx_ref, bx_ref, cx_ref, dtx_ref, (n_lat - 1) * CHUNK, n_ctx + n_lat - 1, False, True)

    for ci in range(n_ctx):
        for direction, cj in ((0, ci), (1, n_ctx - 1 - ci)):
            scan_step(cj, direction, False)

    def body(i, carry, first_touch):
        for direction in range(2):
            chunk = i if direction == 0 else n_lat - 1 - i
            rows = pl.ds(pl.multiple_of(chunk * CHUNK, CHUNK), CHUNK)
            y, x = scan_step(n_ctx + chunk, direction, True)
            if direction == 0:
                y = y + dsk * x.astype(F32)
            if not first_touch:
                y = y + y_ref[rows, :].astype(F32)
            y_ref[rows, :] = y.astype(BF16)
        return carry

    assert n_lat % 2 == 0
    lax.fori_loop(0, n_lat // 2, functools.partial(body, first_touch=True), 0, unroll=4)
    lax.fori_loop(n_lat // 2, n_lat, functools.partial(body, first_touch=False), 0, unroll=4)


def _ssd(uc, dtc, ux, dtx, alog, dsk, conv_w, conv_b):
    bsz, lc, _ = uc.shape
    lx = ux.shape[1]
    n_chunks = (lc + lx) // CHUNK
    b_blk0 = SSD_INNER // SSD_STATE
    c_blk0 = b_blk0 + SSD_GROUPS
    chan_specs = [(GROUP_W, lambda b, g: (0, g)),
                  (SSD_STATE, lambda b, g: (0, b_blk0 + g)),
                  (SSD_STATE, lambda b, g: (0, c_blk0 + g))]

    def seg_specs(rows):
        return [pl.BlockSpec((None, rows, GROUP_W), lambda b, g: (b, 0, g)),
                pl.BlockSpec((None, rows, SSD_STATE), lambda b, g: (b, 0, b_blk0 + g)),
                pl.BlockSpec((None, rows, SSD_STATE), lambda b, g: (b, 0, c_blk0 + g)),
                pl.BlockSpec((None, rows, LANES), lambda b, g: (b, 0, 0))]

    return pl.pallas_call(
        _ssd_kernel,
        grid=(bsz, SSD_GROUPS),
        in_specs=seg_specs(lc) + seg_specs(lx) + [
            pl.BlockSpec((1, LANES), lambda b, g: (0, 0)),
            pl.BlockSpec((1, GROUP_W), lambda b, g: (0, g))]
        + [pl.BlockSpec((SSD_CONV, wd), f) for wd, f in chan_specs]
        + [pl.BlockSpec((1, wd), f) for wd, f in chan_specs],
        out_specs=pl.BlockSpec((None, lx, GROUP_W), lambda b, g: (b, 0, g)),
        out_shape=jax.ShapeDtypeStruct((bsz, lx, SSD_INNER), BF16),
        scratch_shapes=[
            pltpu.VMEM((2, SSD_STATE, GROUP_W), F32),
            pltpu.VMEM((2, n_chunks, CHUNK, LANES), F32),
            pltpu.VMEM((2, n_chunks, 8, CHUNK), F32),
            pltpu.VMEM((2, n_chunks, 8, CHUNK), F32),
            pltpu.VMEM((n_chunks, SSD_STATE, CHUNK), F32),
            pltpu.VMEM((lc + lx, GROUP_W), BF16),
            pltpu.VMEM((lc + lx, SSD_STATE), BF16),
            pltpu.VMEM((lc + lx, SSD_STATE), BF16),
            pltpu.VMEM(((SSD_CONV - 1) * CHUNK, CHUNK + 2 * CONV_PAD), BF16),
        ],
        compiler_params=_cparams(("arbitrary", "arbitrary"), 54 << 20),
        name="ssd",
    )(uc, uc, uc, dtc, ux, ux, ux, dtx, alog, dsk, conv_w, conv_w, conv_w, conv_b, conv_b, conv_b)


def _rope_partner_index():
    j = np.arange(QK_ROPE)
    return np.where(j % 16 < 8, j + 8, j - 8)


def _rope_tables(seq):
    t = np.arange(seq)
    pos = np.stack([t // GRID_W, t % GRID_W], axis=1).astype(np.float64)
    j = np.arange(QK_ROPE)
    inv = ROPE_THETA ** (-np.arange(8, dtype=np.float64) * 2.0 / 16.0)
    ang = pos[:, j // 16] * inv[j % 8][None, :]
    sign = np.where(j % 16 < 8, -1.0, 1.0)
    return np.cos(ang), np.sin(ang) * sign


def _mla_tables(seq, rotate):
    scale = math.log2(math.e) / math.sqrt(QK_NOPE + QK_ROPE)
    if rotate:
        cos, sin = _rope_tables(seq)
    else:
        cos, sin = np.ones((seq, QK_ROPE)), np.zeros((seq, QK_ROPE))
    one = np.ones((seq, QK_NOPE))
    zero = np.zeros((seq, QK_NOPE))
    tabs = {"q": np.concatenate([one, cos, sin], axis=1) * scale,
            "kc": np.concatenate([zero, cos, cos], axis=1),
            "ks": np.concatenate([zero, sin, sin], axis=1)}
    return {k: jnp.asarray(v.astype(np.float32)) for k, v in tabs.items()}


def _mla_weights(w_in, q_norm, w_uq, kv_norm, w_ukv):
    partner = _rope_partner_index()
    kr = w_in[:, Q_LORA + KV_LORA:]
    krp = kr[:, partner]
    zero = jnp.zeros((D_MODEL, QK_NOPE), F32)
    win = jnp.concatenate([w_in[:, :Q_LORA + KV_LORA], zero, kr, kr, zero, krp, krp], axis=1)
    uq = w_uq.reshape(Q_LORA, MLA_HEADS, QK_NOPE + QK_ROPE)
    rope = uq[:, :, QK_NOPE:]
    wq = jnp.concatenate([uq, rope[:, :, partner]], axis=2).reshape(Q_LORA, MLA_HEADS * HEAD_LANES)
    ukv = w_ukv.reshape(KV_LORA, MLA_HEADS, QK_NOPE + V_HEAD)
    wk = jnp.concatenate([ukv[:, :, :QK_NOPE], jnp.zeros((KV_LORA, MLA_HEADS, QK_NOPE), F32)], axis=2)
    return {"win": win.astype(BF16), "qg": q_norm.reshape(1, -1), "wq": wq.astype(BF16),
            "kvg": kv_norm.reshape(1, -1), "wk": wk.reshape(KV_LORA, -1).astype(BF16),
            "wv": ukv[:, :, QK_NOPE:].reshape(KV_LORA, -1).astype(BF16)}


def _post_weights(wo, gf, ffn, layer):
    return {"wo": wo.astype(BF16), "gf": gf.reshape(1, -1), "layer": layer, **ffn}


def _ssd_in_weights(gain, w_in, dt_bias):
    wd = jnp.pad(w_in[:, SSD_INNER + SSD_CONV_DIM:], ((0, 0), (0, LANES - 2 * SSD_HEADS)))
    wdh = wd.astype(BF16)
    wdl = (wd - wdh.astype(F32)).astype(BF16)
    db = jnp.pad(dt_bias.reshape(1, -1), ((0, 0), (0, LANES - 2 * SSD_HEADS)))
    return {"g": gain.reshape(1, -1), "wz": w_in[:, :SSD_INNER].astype(BF16),
            "wx": w_in[:, SSD_INNER:SSD_INNER + SSD_CONV_DIM].astype(BF16),
            "wdh": wdh, "wdl": wdl, "db": db}


def kernel(x, c, ctx, c_ctx, ada_w, ada_b, norm_mix, norm_ffn, ffn_w1, ffn_w3, ffn_w2,
           mla_w_in, mla_q_norm, mla_w_uq, mla_kv_norm, mla_w_ukv, mla_w_o,
           ssd_w_in, ssd_conv_w, ssd_conv_b, ssd_dt_bias, ssd_a_log, ssd_d, ssd_norm, ssd_w_o,
           final_norm):
    bsz, seq, _ = x.shape
    lc = ctx.shape[1]
    tm = 256

    cc = jnp.zeros((8, D_MODEL), F32).at[:bsz].set(c).at[bsz].set(c_ctx)
    mod = _ada(cc, ada_w, ada_b)
    mod_x = [mod[l, :bsz].reshape(bsz, 6, D_MODEL) for l in range(DEPTH)]
    mod_c = [mod[l, bsz:bsz + 1].reshape(1, 6, D_MODEL) for l in range(DEPTH)]

    wm = _mla_weights(mla_w_in[0], mla_q_norm[0], mla_w_uq[0], mla_kv_norm[0], mla_w_ukv[0])
    g0 = norm_mix[0].reshape(1, -1)
    q_c, k_c, v_c = _mla_proj(ctx, mod_c[0], g0, wm, _mla_tables(lc, False), tm)
    q_x, k_x, v_x = _mla_proj(x, mod_x[0], g0, wm, _mla_tables(seq, True), 2 * tm)
    o_x = _attn(q_x, [(k_c, v_c), (k_x, v_x)], tq=256, tk=512)
    o_c = _attn(q_c, [(k_c, v_c)], tq=lc, tk=512)
    ffn = {"w1": ffn_w1.astype(BF16), "w3": ffn_w3.astype(BF16), "w2": ffn_w2.astype(BF16)}
    wp0 = _post_weights(mla_w_o[0], norm_ffn[0], ffn, 0)
    x = _post_mla(x, o_x, mod_x[0], wp0, 2 * tm)
    ctx = _post_mla(ctx, o_c, mod_c[0], wp0, tm)

    ws = _ssd_in_weights(norm_mix[1], ssd_w_in[0], ssd_dt_bias[0])
    _, xbc_c, dt_c = _ssd_in(ctx, mod_c[1], ws, tm)
    z_x, xbc_x, dt_x = _ssd_in(x, mod_x[1], ws, 2 * tm)
    alog = jnp.pad(ssd_a_log[0].reshape(1, -1), ((0, 0), (0, LANES - 2 * SSD_HEADS)))
    dsk = jnp.repeat(ssd_d[0], SSD_HEADDIM).reshape(1, -1)
    y = _ssd(xbc_c, dt_c, xbc_x, dt_x, alog, dsk, ssd_conv_w[0], ssd_conv_b[0].reshape(1, -1))
    wp1 = _post_weights(ssd_w_o[0], norm_ffn[1], ffn, 1)
    return _post_ssd(x, y, z_x, ssd_norm[0].reshape(1, -1), mod_x[1], wp1,
                     final_norm.reshape(1, -1), 2 * tm)
```

```python
import functools
import math

import numpy as np
import jax
import jax.numpy as jnp
from jax import lax
from jax.experimental import pallas as pl
from jax.experimental.pallas import tpu as pltpu

D_MODEL = 1024
DEPTH = 2
GRID_W = 64
EPS = 1e-6
MLA_HEADS = 16
Q_LORA = 512
KV_LORA = 256
QK_NOPE = 64
QK_ROPE = 32
V_HEAD = 64
ROPE_THETA = 10000.0
SSD_INNER = 2 * D_MODEL
SSD_HEADDIM = 64
SSD_HEADS = SSD_INNER // SSD_HEADDIM
SSD_GROUPS = 4
SSD_STATE = 128
SSD_CONV = 5
SSD_CONV_DIM = SSD_INNER + 2 * SSD_GROUPS * SSD_STATE
CHUNK = 128
FFN_HIDDEN = ((8 * D_MODEL // 3 + 255) // 256) * 256

LANES = 128
V7X_VMEM_BYTES = 64 * 1024 * 1024
VMEM_LIMIT_CAP = 56 * 1024 * 1024

F32 = jnp.float32
BF16 = jnp.bfloat16
HEAD_LANES = 2 * QK_NOPE
HEADS_PER_GROUP = SSD_HEADS // SSD_GROUPS
GROUP_W = HEADS_PER_GROUP * SSD_HEADDIM


def _cparams(semantics, vmem_bytes):
    return pltpu.CompilerParams(
        dimension_semantics=semantics,
        vmem_limit_bytes=int(min(VMEM_LIMIT_CAP, vmem_bytes)),
    )


def _const_spec(shape):
    nd = len(shape)
    return pl.BlockSpec(shape, lambda *_: (0,) * nd, pipeline_mode=pl.Buffered(1))


def _rms(x, g):
    return x * lax.rsqrt(jnp.mean(x * x, axis=-1, keepdims=True) + EPS) * g


def _silu(x):
    return x * (1.0 / (1.0 + jnp.exp(-x)))


def _dot(a, b):
    return jnp.dot(a, b, preferred_element_type=F32)


def _dot_nt(a, b):
    return lax.dot_general(a, b, (((1,), (1,)), ((), ())), preferred_element_type=F32)


def _ada_kernel(c_ref, w_ref, b_ref, o_ref):
    s = _silu(c_ref[...])
    s_hi = s.astype(BF16)
    s_lo = (s - s_hi.astype(F32)).astype(BF16)
    w = w_ref[...]
    w_hi = w.astype(BF16)
    w_lo = (w - w_hi.astype(F32)).astype(BF16)
    r = _dot(jnp.concatenate([s_hi, s_lo], axis=0), w_hi)
    o_ref[...] = r[0:8, :] + r[8:16, :] + _dot(s_hi, w_lo) + b_ref[...]


def _ada(cc, ada_w, ada_b):
    width = ada_w.shape[2] // 2
    return pl.pallas_call(
        _ada_kernel,
        grid=(DEPTH, 2),
        in_specs=[
            pl.BlockSpec((8, D_MODEL), lambda l, j: (0, 0)),
            pl.BlockSpec((None, D_MODEL, width), lambda l, j: (l, 0, j)),
            pl.BlockSpec((None, 1, width), lambda l, j: (l, 0, j)),
        ],
        out_specs=pl.BlockSpec((None, 8, width), lambda l, j: (l, 0, j)),
        out_shape=jax.ShapeDtypeStruct((DEPTH, 8, ada_w.shape[2]), F32),
        compiler_params=_cparams(("arbitrary", "arbitrary"), 2 * D_MODEL * width * 4 + (16 << 20)),
        name="ada",
    )(cc, ada_w, ada_b.reshape(DEPTH, 1, -1))


def _mod_spec(mod):
    if mod.shape[0] == 1:
        return pl.BlockSpec((None, 6, D_MODEL), lambda b, i: (0, 0, 0))
    return pl.BlockSpec((None, 6, D_MODEL), lambda b, i: (b, 0, 0))


def _mla_proj_kernel(x_ref, mod_ref, g_ref, win_ref, qg_ref, wq_ref, kvg_ref, wk_ref, wv_ref,
                     tq_ref, tkc_ref, tks_ref, q_ref, k_ref, v_ref, *, sub):
    tiles = [slice(r0, r0 + sub) for r0 in range(0, x_ref.shape[0], sub)]
    hs = [(_rms(x_ref[rows, :], g_ref[...]) * (1.0 + mod_ref[1:2, :]) + mod_ref[0:1, :]).astype(BF16)
          for rows in tiles]
    ps = [_dot(h, win_ref[...]) for h in hs]
    cqns = [_rms(p[:, :Q_LORA], qg_ref[...]).astype(BF16) for p in ps]
    ckvns = [_rms(p[:, Q_LORA:Q_LORA + KV_LORA], kvg_ref[...]).astype(BF16) for p in ps]
    for rows, p, cqn, ckvn in zip(tiles, ps, cqns, ckvns):
        kr_a = p[:, Q_LORA + KV_LORA:Q_LORA + KV_LORA + LANES]
        kr_b = p[:, Q_LORA + KV_LORA + LANES:]
        qf = _dot(cqn, wq_ref[...])
        kf = _dot(ckvn, wk_ref[...])
        kr = kr_a * tkc_ref[rows, :] + kr_b * tks_ref[rows, :]
        tq = tq_ref[rows, :]
        for hd in range(MLA_HEADS):
            sl = slice(hd * HEAD_LANES, (hd + 1) * HEAD_LANES)
            q_ref[hd, rows, :] = (qf[:, sl] * tq).astype(BF16)
            k_ref[hd, rows, :] = (kf[:, sl] + kr).astype(BF16)
        v_ref[rows, :] = _dot(ckvn, wv_ref[...]).astype(BF16)


def _mla_proj(x, mod, gain, w, tabs, tm, sub=256):
    bsz, seq, _ = x.shape
    row = lambda n: pl.BlockSpec((None, tm, n), lambda b, i: (b, i, 0))
    heads = pl.BlockSpec((None, MLA_HEADS, tm, HEAD_LANES), lambda b, i: (b, 0, i, 0))
    tab = pl.BlockSpec((tm, LANES), lambda b, i: (i, 0))
    return pl.pallas_call(
        functools.partial(_mla_proj_kernel, sub=min(sub, tm)),
        grid=(bsz, seq // tm),
        in_specs=[
            row(D_MODEL), _mod_spec(mod), _const_spec((1, D_MODEL)),
            _const_spec(w["win"].shape), _const_spec((1, Q_LORA)), _const_spec(w["wq"].shape),
            _const_spec((1, KV_LORA)), _const_spec(w["wk"].shape), _const_spec(w["wv"].shape),
            tab, tab, tab,
        ],
        out_specs=[heads, heads, row(MLA_HEADS * V_HEAD)],
        out_shape=[
            jax.ShapeDtypeStruct((bsz, MLA_HEADS, seq, HEAD_LANES), BF16),
            jax.ShapeDtypeStruct((bsz, MLA_HEADS, seq, HEAD_LANES), BF16),
            jax.ShapeDtypeStruct((bsz, seq, MLA_HEADS * V_HEAD), BF16),
        ],
        compiler_params=_cparams(("arbitrary", "arbitrary"), 40 << 20),
        name="mla_proj",
    )(x, mod, gain, w["win"], w["qg"], w["wq"], w["kvg"], w["wk"], w["wv"],
      tabs["q"], tabs["kc"], tabs["ks"])


def _attn_kernel(*refs, n_kv, tq, tk):
    q_ref = refs[0]
    kv_refs = [(refs[1 + 2 * j], refs[2 + 2 * j]) for j in range(n_kv)]
    o_ref = refs[1 + 2 * n_kv]
    s_ref, m_ref, vm_ref = refs[2 + 2 * n_kv:]
    n_q = q_ref.shape[1] // tq
    chan = lax.broadcasted_iota(jnp.int32, (2 * V_HEAD, 1), 0)
    low_half = chan < V_HEAD
    chunks = []
    col = 0
    for k_ref, v_ref in kv_refs:
        rows = k_ref.shape[1]
        v_t = v_ref[...].astype(F32).T
        vm_ref[0, :, col:col + rows] = jnp.where(low_half, v_t, 1.0).astype(BF16)
        vm_ref[1, :, col:col + rows] = jnp.where(low_half, 1.0, v_t).astype(BF16)
        step = min(tk, rows)
        for r0 in range(0, rows, step):
            chunks.append((k_ref, r0, step, col + r0))
        col += rows

    def tile_rows(i):
        start = i * tq
        return pl.ds(start if isinstance(start, int) else pl.multiple_of(start, tq), tq)

    def scores(i, hh):
        qh = q_ref[hh, tile_rows(i), :]
        m = None
        for k_ref, r0, rows, c0 in chunks:
            s_t = _dot_nt(k_ref[hh, r0:r0 + rows, :], qh)
            s_ref[hh, c0:c0 + rows, :] = s_t
            cm = jnp.max(s_t, axis=0, keepdims=True)
            m = cm if m is None else jnp.maximum(m, cm)
        m_ref[hh] = jnp.broadcast_to(m, (8, tq))

    def softmax_pv(hh):
        acc = jnp.zeros((2 * V_HEAD, tq), F32)
        for _, _, rows, c0 in chunks:
            p_t = jnp.exp2(s_ref[hh, c0:c0 + rows, :] - m_ref[hh, 0:1, :]).astype(BF16)
            acc = acc + _dot(vm_ref[hh, :, c0:c0 + rows], p_t)
        return acc

    def tile(i, next_scores):
        scores(i, 1)
        acc0 = softmax_pv(0)
        if next_scores:
            scores(i + 1, 0)
        acc1 = softmax_pv(1)
        out_t = jnp.where(low_half, acc0 / acc0[V_HEAD:V_HEAD + 1, :], acc1 / acc1[0:1, :])
        o_ref[tile_rows(i), :] = out_t.T.astype(BF16)

    scores(0, 0)

    def body(i, carry):
        tile(i, True)
        return carry

    lax.fori_loop(0, n_q - 1, body, 0)
    tile(n_q - 1, False)


def _attn(q, kvs, tq, tk):
    bsz, _, lq, _ = q.shape
    n_pairs = MLA_HEADS // 2
    total = sum(k.shape[2] for k, _ in kvs)
    in_specs = [pl.BlockSpec((None, 2, lq, HEAD_LANES), lambda b, h: (b, h, 0, 0))]
    args = [q]
    for k, v in kvs:
        t = k.shape[2]
        in_specs.append(pl.BlockSpec((None, 2, t, HEAD_LANES), lambda b, h: (b, h, 0, 0)))
        in_specs.append(pl.BlockSpec((None, t, 2 * V_HEAD), lambda b, h: (b, 0, h)))
        args += [k, v]
    blocks = 2 * (2 * lq * HEAD_LANES + total * (2 * HEAD_LANES + 2 * V_HEAD) + lq * 2 * V_HEAD) * 2
    vmem = blocks + 2 * tq * (total + LANES) * 4 + 2 * total * 2 * V_HEAD * 2 + (16 << 20)
    return pl.pallas_call(
        functools.partial(_attn_kernel, n_kv=len(kvs), tq=tq, tk=tk),
        grid=(bsz, n_pairs),
        in_specs=in_specs,
        out_specs=pl.BlockSpec((None, lq, 2 * V_HEAD), lambda b, h: (b, 0, h)),
        out_shape=jax.ShapeDtypeStruct((bsz, lq, MLA_HEADS * V_HEAD), BF16),
        scratch_shapes=[pltpu.VMEM((2, total, tq), F32), pltpu.VMEM((2, 8, tq), F32),
                        pltpu.VMEM((2, 2 * V_HEAD, total), BF16)],
        compiler_params=_cparams(("arbitrary", "arbitrary"), vmem),
        name="attn",
    )(*args)


def _resid_ffn(mix, x_ref, mod_ref, wo_ref, gf_ref, w1_ref, w3_ref, w2_ref, fg_ref, o_ref, sub):
    tiles = [slice(r0, r0 + sub) for r0 in range(0, x_ref.shape[0], sub)]
    a = [mix(rows) for rows in tiles]
    x1 = [x_ref[rows, :] + mod_ref[2:3, :] * _dot(ak, wo_ref[...]) for rows, ak in zip(tiles, a)]
    hx = [(_rms(xk, gf_ref[...]) * (1.0 + mod_ref[4:5, :]) + mod_ref[3:4, :]).astype(BF16) for xk in x1]
    t = [(_silu(_dot(hk, w1_ref[...])) * _dot(hk, w3_ref[...])).astype(BF16) for hk in hx]
    for rows, xk, tk_ in zip(tiles, x1, t):
        x2 = xk + mod_ref[5:6, :] * _dot(tk_, w2_ref[...])
        if fg_ref is not None:
            x2 = _rms(x2, fg_ref[...])
        o_ref[rows, :] = x2


def _post_mla_kernel(x_ref, a_ref, mod_ref, wo_ref, gf_ref, w1_ref, w3_ref, w2_ref, o_ref, *, sub):
    _resid_ffn(lambda rows: a_ref[rows, :], x_ref, mod_ref, wo_ref, gf_ref, w1_ref, w3_ref, w2_ref,
               None, o_ref, sub)


def _post_ssd_kernel(x_ref, y_ref, z_ref, ng_ref, mod_ref, wo_ref, gf_ref, w1_ref, w3_ref, w2_ref,
                     fg_ref, o_ref, *, sub):
    def gated_norm(rows):
        gated = y_ref[rows, :].astype(F32) * _silu(z_ref[rows, :].astype(F32))
        return _rms(gated, ng_ref[...]).astype(BF16)

    _resid_ffn(gated_norm, x_ref, mod_ref, wo_ref, gf_ref, w1_ref, w3_ref, w2_ref, fg_ref, o_ref, sub)


def _layer_spec(stacked, layer):
    return pl.BlockSpec((None,) + stacked.shape[1:], lambda *_: (layer, 0, 0),
                        pipeline_mode=pl.Buffered(1))


def _post_weight_specs(w):
    layer = w["layer"]
    return [_const_spec(w["wo"].shape), _const_spec((1, D_MODEL)), _layer_spec(w["w1"], layer),
            _layer_spec(w["w3"], layer), _layer_spec(w["w2"], layer)]


def _post_mla(x, a, mod, w, tm, sub=256):
    bsz, seq, _ = x.shape
    row = lambda n: pl.BlockSpec((None, tm, n), lambda b, i: (b, i, 0))
    return pl.pallas_call(
        functools.partial(_post_mla_kernel, sub=min(sub, tm)),
        grid=(bsz, seq // tm),
        in_specs=[row(D_MODEL), row(a.shape[2]), _mod_spec(mod)] + _post_weight_specs(w),
        out_specs=row(D_MODEL),
        out_shape=jax.ShapeDtypeStruct((bsz, seq, D_MODEL), F32),
        compiler_params=_cparams(("arbitrary", "arbitrary"), VMEM_LIMIT_CAP),
        name="post_mla",
    )(x, a, mod, w["wo"], w["gf"], w["w1"], w["w3"], w["w2"])


def _post_ssd(x, y, z, norm_gain, mod, w, final_gain, tm, sub=256):
    bsz, seq, _ = x.shape
    row = lambda n: pl.BlockSpec((None, tm, n), lambda b, i: (b, i, 0))
    return pl.pallas_call(
        functools.partial(_post_ssd_kernel, sub=min(sub, tm)),
        grid=(bsz, seq // tm),
        in_specs=[row(D_MODEL), row(SSD_INNER), row(SSD_INNER), _const_spec((1, SSD_INNER)),
                  _mod_spec(mod)] + _post_weight_specs(w) + [_const_spec((1, D_MODEL))],
        out_specs=row(D_MODEL),
        out_shape=jax.ShapeDtypeStruct((bsz, seq, D_MODEL), F32),
        compiler_params=_cparams(("arbitrary", "arbitrary"), VMEM_LIMIT_CAP),
        name="post_ssd",
    )(x, y, z, norm_gain, mod, w["wo"], w["gf"], w["w1"], w["w3"], w["w2"], final_gain)


def _ssd_in_kernel(x_ref, mod_ref, g_ref, wz_ref, wx_ref, wdh_ref, wdl_ref, db_ref,
                   z_ref, xbc_ref, dt_ref, *, sub):
    tiles = [slice(r0, r0 + sub) for r0 in range(0, x_ref.shape[0], sub)]
    hs = [_rms(x_ref[rows, :], g_ref[...]) * (1.0 + mod_ref[1:2, :]) + mod_ref[0:1, :] for rows in tiles]
    hbs = [h.astype(BF16) for h in hs]
    for rows, hb in zip(tiles, hbs):
        z_ref[rows, :] = _dot(hb, wz_ref[...]).astype(BF16)
    for rows, hb in zip(tiles, hbs):
        xbc_ref[rows, :] = _dot(hb, wx_ref[...]).astype(BF16)
    for rows, h, hb in zip(tiles, hs, hbs):
        hl = (h - hb.astype(F32)).astype(BF16)
        raw = _dot(hb, wdh_ref[...]) + _dot(hl, wdh_ref[...]) + _dot(hb, wdl_ref[...])
        raw = raw + db_ref[...]
        dt_ref[rows, :] = jnp.maximum(raw, 0.0) + jnp.log(1.0 + jnp.exp(-jnp.abs(raw)))


def _ssd_in(x, mod, w, tm, sub=256):
    bsz, seq, _ = x.shape
    row = lambda n: pl.BlockSpec((None, tm, n), lambda b, i: (b, i, 0))
    return pl.pallas_call(
        functools.partial(_ssd_in_kernel, sub=min(sub, tm)),
        grid=(bsz, seq // tm),
        in_specs=[row(D_MODEL), _mod_spec(mod), _const_spec((1, D_MODEL)),
                  _const_spec(w["wz"].shape), _const_spec(w["wx"].shape),
                  _const_spec(w["wdh"].shape), _const_spec(w["wdl"].shape),
                  _const_spec((1, LANES))],
        out_specs=[row(SSD_INNER), row(SSD_CONV_DIM), row(LANES)],
        out_shape=[jax.ShapeDtypeStruct((bsz, seq, SSD_INNER), BF16),
                   jax.ShapeDtypeStruct((bsz, seq, SSD_CONV_DIM), BF16),
                   jax.ShapeDtypeStruct((bsz, seq, LANES), F32)],
        compiler_params=_cparams(("arbitrary", "arbitrary"), 48 << 20),
        name="ssd_in",
    )(x, mod, w["g"], w["wz"], w["wx"], w["wdh"], w["wdl"], w["db"])


CONV_PAD = 16
LOG2E = math.log2(math.e)


def _ssd_kernel(xc_ref, bc_ref, cc_ref, dtc_ref, xx_ref, bx_ref, cx_ref, dtx_ref,
                alog_ref, dsk_ref, cwx_ref, cwb_ref, cwc_ref, cbx_ref, cbb_ref, cbc_ref,
                y_ref, s_ref, cum_ref, cumt_ref, dtt_ref, btr_ref, ux_ref, ub_ref, uc_ref, shift_ref):
    g = pl.program_id(1)
    n_ctx = xc_ref.shape[0] // CHUNK
    n_lat = xx_ref.shape[0] // CHUNK
    a_all = jnp.broadcast_to(-jnp.exp(alog_ref[...]), (8, LANES))
    qi = lax.broadcasted_iota(jnp.int32, (CHUNK, CHUNK), 0)
    ki = lax.broadcasted_iota(jnp.int32, (CHUNK, CHUNK), 1)
    lane = lax.broadcasted_iota(jnp.int32, (1, LANES), 1)
    low_half = lane < SSD_HEADDIM
    n_pair = HEADS_PER_GROUP // 2

    def pair_cols(v, pr):
        return jnp.where(low_half, v[:, 2 * pr:2 * pr + 1], v[:, 2 * pr + 1:2 * pr + 2])

    def split3(v):
        hi = v.astype(BF16)
        r1 = v - hi.astype(F32)
        mid = r1.astype(BF16)
        return hi, mid, (r1 - mid.astype(F32)).astype(BF16)

    def lane_bcast(v, col):
        return jnp.broadcast_to(v[:, col:col + 1], (v.shape[0], LANES))

    sel_r = lax.broadcasted_iota(jnp.int32, (CHUNK, CHUNK + 2 * CONV_PAD), 0)
    sel_c = lax.broadcasted_iota(jnp.int32, (CHUNK, CHUNK + 2 * CONV_PAD), 1)
    shifted_taps = [j for j in range(SSD_CONV) if j != SSD_CONV // 2]
    for n, j in enumerate(shifted_taps):
        hit = sel_c == sel_r + (CONV_PAD - SSD_CONV // 2 + j)
        shift_ref[n * CHUNK:(n + 1) * CHUNK, :] = jnp.where(hit, 1.0, 0.0).astype(BF16)

    def window(raw_ref, r0, first, last):
        pad = jnp.zeros((CONV_PAD, raw_ref.shape[1]), BF16)
        if first and last:
            return jnp.concatenate([pad, raw_ref[0:CHUNK, :], pad], axis=0)
        if first:
            return jnp.concatenate([pad, raw_ref[0:CHUNK + CONV_PAD, :]], axis=0)
        if last:
            return jnp.concatenate([raw_ref[r0 - CONV_PAD:r0 + CHUNK, :], pad], axis=0)
        return raw_ref[pl.ds(pl.multiple_of(r0 - CONV_PAD, CONV_PAD), CHUNK + 2 * CONV_PAD), :]

    def scan_sums(x_ref, b_ref, c_ref, dt_ref, r0, chunk_id, first, last):
        grow = pl.ds(chunk_id * CHUNK if isinstance(chunk_id, int)
                     else pl.multiple_of(chunk_id * CHUNK, CHUNK), CHUNK)
        win = jnp.concatenate([window(r, r0, first, last) for r in (x_ref, b_ref, c_ref)], axis=1)
        taps = _dot(shift_ref[...], win)
        cw = jnp.concatenate([cwx_ref[...], cwb_ref[...], cwc_ref[...]], axis=1)
        acc = jnp.concatenate([cbx_ref[...], cbb_ref[...], cbc_ref[...]], axis=1)
        mid = SSD_CONV // 2
        acc = acc + win[CONV_PAD:CONV_PAD + CHUNK, :].astype(F32) * cw[mid:mid + 1, :]
        for n, j in enumerate(shifted_taps):
            acc = acc + taps[n * CHUNK:(n + 1) * CHUNK, :] * cw[j:j + 1, :]
        u = _silu(acc).astype(BF16)
        ub = u[:, GROUP_W:GROUP_W + SSD_STATE]
        ux_ref[grow, :] = u[:, :GROUP_W]
        ub_ref[grow, :] = ub
        uc_ref[grow, :] = u[:, GROUP_W + SSD_STATE:]
        btr_ref[chunk_id] = ub.astype(F32).T
        dt = dt_ref[pl.ds(r0, CHUNK), :]
        for direction in range(2):
            causal = (ki <= qi) if direction == 0 else (ki >= qi)
            shift = (LANES - (direction * SSD_HEADS + g * HEADS_PER_GROUP)) & (LANES - 1)
            dts = pltpu.roll(dt, shift, 1)
            a = dts * pltpu.roll(a_all, shift, 1)[0:1, :]
            tri = jnp.where(causal, 1.0, 0.0).astype(BF16)
            r = _dot(tri, jnp.concatenate(split3(a), axis=1))
            c = (r[:, :LANES] + r[:, LANES:2 * LANES] + r[:, 2 * LANES:]) * LOG2E
            cum_ref[direction, chunk_id] = c
            cumt_ref[direction, chunk_id] = c.T[0:8, :]
            dtt_ref[direction, chunk_id] = dts.T[0:8, :]

    def scan_step(chunk_id, direction, emit):
        grow = pl.ds(chunk_id * CHUNK if isinstance(chunk_id, int)
                     else pl.multiple_of(chunk_id * CHUNK, CHUNK), CHUNK)
        st_ref = s_ref.at[direction]
        causal = (ki <= qi) if direction == 0 else (ki >= qi)
        c = cum_ref[direction, chunk_id]
        c_t = cumt_ref[direction, chunk_id]
        dt_t = dtt_ref[direction, chunk_id]
        end = CHUNK - 1 if direction == 0 else 0
        to_end_t = jnp.exp2(c_t[:, end:end + 1] - c_t) * dt_t
        cd = jnp.exp2(c[end:end + 1, :])
        b_t = btr_ref[chunk_id]
        x = ux_ref[grow, :]
        if emit:
            cmat = uc_ref[grow, :]
            cb = _dot_nt(cmat, ub_ref[grow, :])
            y_off = _dot(cmat, st_ref[...].astype(BF16))
        ys, upds = [], []
        for pr in range(n_pair):
            heads = (2 * pr, 2 * pr + 1)
            ps = slice(pr * LANES, (pr + 1) * LANES)
            xp = x[:, ps]
            zero = jnp.zeros_like(xp)
            rhs = jnp.concatenate([jnp.where(low_half, xp, zero),
                                   jnp.where(low_half, zero, xp)], axis=0)
            bts = jnp.concatenate([(b_t * to_end_t[hd:hd + 1, :]).astype(BF16) for hd in heads], axis=1)
            upds.append(_dot(bts, rhs))
            if emit:
                ws, e_b = [], []
                for hd in heads:
                    cq = lane_bcast(c, hd)
                    dec = jnp.exp2(jnp.where(causal, cq - c_t[hd:hd + 1, :], -jnp.inf))
                    ws.append((cb * dec * dt_t[hd:hd + 1, :]).astype(BF16))
                    e_b.append(jnp.exp2(cq))
                ys.append(_dot(jnp.concatenate(ws, axis=1), rhs)
                          + jnp.where(low_half, e_b[0], e_b[1]) * y_off[:, ps])
        cds = jnp.concatenate([pair_cols(cd, pr) for pr in range(n_pair)], axis=1)
        st_ref[...] = st_ref[...] * cds + jnp.concatenate(upds, axis=1)
        return (jnp.concatenate(ys, axis=1), x) if emit else (None, None)

    dsk = dsk_ref[...]
    s_ref[...] = jnp.zeros_like(s_ref)
    for ci in range(n_ctx):
        scan_sums(xc_ref, bc_ref, cc_ref, dtc_ref, ci * CHUNK, ci, ci == 0, ci == n_ctx - 1)

    assert n_lat >= 2
    scan_sums(xx_ref, bx_ref, cx_ref, dtx_ref, 0, n_ctx, True, False)

    def sums_body(j, carry):
        scan_sums(xx_ref, bx_ref, cx_ref, dtx_ref, pl.multiple_of(j * CHUNK, CHUNK), n_ctx + j,
                  False, False)
        return carry

    lax.fori_loop(1, n_lat - 1, sums_body, 0, unroll=2)
    scan_sums(xx_ref, bx_ref, cx_ref, dtx_ref, (n_lat - 1) * CHUNK, n_ctx + n_lat - 1, False, True)

    for ci in range(n_ctx):
        for direction, cj in ((0, ci), (1, n_ctx - 1 - ci)):
            scan_step(cj, direction, False)

    def body(i, carry, first_touch):
        for direction in range(2):
            chunk = i if direction == 0 else n_lat - 1 - i
            rows = pl.ds(pl.multiple_of(chunk * CHUNK, CHUNK), CHUNK)
            y, x = scan_step(n_ctx + chunk, direction, True)
            if direction == 0:
                y = y + dsk * x.astype(F32)
            if not first_touch:
                y = y + y_ref[rows, :].astype(F32)
            y_ref[rows, :] = y.astype(BF16)
        return carry

    assert n_lat % 2 == 0
    lax.fori_loop(0, n_lat // 2, functools.partial(body, first_touch=True), 0, unroll=4)
    lax.fori_loop(n_lat // 2, n_lat, functools.partial(body, first_touch=False), 0, unroll=4)


def _ssd(uc, dtc, ux, dtx, alog, dsk, conv_w, conv_b):
    bsz, lc, _ = uc.shape
    lx = ux.shape[1]
    n_chunks = (lc + lx) // CHUNK
    b_blk0 = SSD_INNER // SSD_STATE
    c_blk0 = b_blk0 + SSD_GROUPS
    chan_specs = [(GROUP_W, lambda b, g: (0, g)),
                  (SSD_STATE, lambda b, g: (0, b_blk0 + g)),
                  (SSD_STATE, lambda b, g: (0, c_blk0 + g))]

    def seg_specs(rows):
        return [pl.BlockSpec((None, rows, GROUP_W), lambda b, g: (b, 0, g)),
                pl.BlockSpec((None, rows, SSD_STATE), lambda b, g: (b, 0, b_blk0 + g)),
                pl.BlockSpec((None, rows, SSD_STATE), lambda b, g: (b, 0, c_blk0 + g)),
                pl.BlockSpec((None, rows, LANES), lambda b, g: (b, 0, 0))]

    return pl.pallas_call(
        _ssd_kernel,
        grid=(bsz, SSD_GROUPS),
        in_specs=seg_specs(lc) + seg_specs(lx) + [
            pl.BlockSpec((1, LANES), lambda b, g: (0, 0)),
            pl.BlockSpec((1, GROUP_W), lambda b, g: (0, g))]
        + [pl.BlockSpec((SSD_CONV, wd), f) for wd, f in chan_specs]
        + [pl.BlockSpec((1, wd), f) for wd, f in chan_specs],
        out_specs=pl.BlockSpec((None, lx, GROUP_W), lambda b, g: (b, 0, g)),
        out_shape=jax.ShapeDtypeStruct((bsz, lx, SSD_INNER), BF16),
        scratch_shapes=[
            pltpu.VMEM((2, SSD_STATE, GROUP_W), F32),
            pltpu.VMEM((2, n_chunks, CHUNK, LANES), F32),
            pltpu.VMEM((2, n_chunks, 8, CHUNK), F32),
            pltpu.VMEM((2, n_chunks, 8, CHUNK), F32),
            pltpu.VMEM((n_chunks, SSD_STATE, CHUNK), F32),
            pltpu.VMEM((lc + lx, GROUP_W), BF16),
            pltpu.VMEM((lc + lx, SSD_STATE), BF16),
            pltpu.VMEM((lc + lx, SSD_STATE), BF16),
            pltpu.VMEM(((SSD_CONV - 1) * CHUNK, CHUNK + 2 * CONV_PAD), BF16),
        ],
        compiler_params=_cparams(("arbitrary", "arbitrary"), 54 << 20),
        name="ssd",
    )(uc, uc, uc, dtc, ux, ux, ux, dtx, alog, dsk, conv_w, conv_w, conv_w, conv_b, conv_b, conv_b)


def _rope_partner_index():
    j = np.arange(QK_ROPE)
    return np.where(j % 16 < 8, j + 8, j - 8)


def _rope_tables(seq):
    t = np.arange(seq)
    pos = np.stack([t // GRID_W, t % GRID_W], axis=1).astype(np.float64)
    j = np.arange(QK_ROPE)
    inv = ROPE_THETA ** (-np.arange(8, dtype=np.float64) * 2.0 / 16.0)
    ang = pos[:, j // 16] * inv[j % 8][None, :]
    sign = np.where(j % 16 < 8, -1.0, 1.0)
    return np.cos(ang), np.sin(ang) * sign


def _mla_tables(seq, rotate):
    scale = math.log2(math.e) / math.sqrt(QK_NOPE + QK_ROPE)
    if rotate:
        cos, sin = _rope_tables(seq)
    else:
        cos, sin = np.ones((seq, QK_ROPE)), np.zeros((seq, QK_ROPE))
    one = np.ones((seq, QK_NOPE))
    zero = np.zeros((seq, QK_NOPE))
    tabs = {"q": np.concatenate([one, cos, sin], axis=1) * scale,
            "kc": np.concatenate([zero, cos, cos], axis=1),
            "ks": np.concatenate([zero, sin, sin], axis=1)}
    return {k: jnp.asarray(v.astype(np.float32)) for k, v in tabs.items()}


def _mla_weights(w_in, q_norm, w_uq, kv_norm, w_ukv):
    partner = _rope_partner_index()
    kr = w_in[:, Q_LORA + KV_LORA:]
    krp = kr[:, partner]
    zero = jnp.zeros((D_MODEL, QK_NOPE), F32)
    win = jnp.concatenate([w_in[:, :Q_LORA + KV_LORA], zero, kr, kr, zero, krp, krp], axis=1)
    uq = w_uq.reshape(Q_LORA, MLA_HEADS, QK_NOPE + QK_ROPE)
    rope = uq[:, :, QK_NOPE:]
    wq = jnp.concatenate([uq, rope[:, :, partner]], axis=2).reshape(Q_LORA, MLA_HEADS * HEAD_LANES)
    ukv = w_ukv.reshape(KV_LORA, MLA_HEADS, QK_NOPE + V_HEAD)
    wk = jnp.concatenate([ukv[:, :, :QK_NOPE], jnp.zeros((KV_LORA, MLA_HEADS, QK_NOPE), F32)], axis=2)
    return {"win": win.astype(BF16), "qg": q_norm.reshape(1, -1), "wq": wq.astype(BF16),
            "kvg": kv_norm.reshape(1, -1), "wk": wk.reshape(KV_LORA, -1).astype(BF16),
            "wv": ukv[:, :, QK_NOPE:].reshape(KV_LORA, -1).astype(BF16)}


def _post_weights(wo, gf, ffn, layer):
    return {"wo": wo.astype(BF16), "gf": gf.reshape(1, -1), "layer": layer, **ffn}


def _ssd_in_weights(gain, w_in, dt_bias):
    wd = jnp.pad(w_in[:, SSD_INNER + SSD_CONV_DIM:], ((0, 0), (0, LANES - 2 * SSD_HEADS)))
    wdh = wd.astype(BF16)
    wdl = (wd - wdh.astype(F32)).astype(BF16)
    db = jnp.pad(dt_bias.reshape(1, -1), ((0, 0), (0, LANES - 2 * SSD_HEADS)))
    return {"g": gain.reshape(1, -1), "wz": w_in[:, :SSD_INNER].astype(BF16),
            "wx": w_in[:, SSD_INNER:SSD_INNER + SSD_CONV_DIM].astype(BF16),
            "wdh": wdh, "wdl": wdl, "db": db}


def kernel(x, c, ctx, c_ctx, ada_w, ada_b, norm_mix, norm_ffn, ffn_w1, ffn_w3, ffn_w2,
           mla_w_in, mla_q_norm, mla_w_uq, mla_kv_norm, mla_w_ukv, mla_w_o,
           ssd_w_in, ssd_conv_w, ssd_conv_b, ssd_dt_bias, ssd_a_log, ssd_d, ssd_norm, ssd_w_o,
           final_norm):
    bsz, seq, _ = x.shape
    lc = ctx.shape[1]
    tm = 256

    cc = jnp.zeros((8, D_MODEL), F32).at[:bsz].set(c).at[bsz].set(c_ctx)
    mod = _ada(cc, ada_w, ada_b)
    mod_x = [mod[l, :bsz].reshape(bsz, 6, D_MODEL) for l in range(DEPTH)]
    mod_c = [mod[l, bsz:bsz + 1].reshape(1, 6, D_MODEL) for l in range(DEPTH)]

    wm = _mla_weights(mla_w_in[0], mla_q_norm[0], mla_w_uq[0], mla_kv_norm[0], mla_w_ukv[0])
    g0 = norm_mix[0].reshape(1, -1)
    q_c, k_c, v_c = _mla_proj(ctx, mod_c[0], g0, wm, _mla_tables(lc, False), tm)
    q_x, k_x, v_x = _mla_proj(x, mod_x[0], g0, wm, _mla_tables(seq, True), 2 * tm)
    o_x = _attn(q_x, [(k_c, v_c), (k_x, v_x)], tq=256, tk=512)
    o_c = _attn(q_c, [(k_c, v_c)], tq=lc, tk=512)
    ffn = {"w1": ffn_w1.astype(BF16), "w3": ffn_w3.astype(BF16), "w2": ffn_w2.astype(BF16)}
    wp0 = _post_weights(mla_w_o[0], norm_ffn[0], ffn, 0)
    x = _post_mla(x, o_x, mod_x[0], wp0, 2 * tm)
    ctx = _post_mla(ctx, o_c, mod_c[0], wp0, tm)

    ws = _ssd_in_weights(norm_mix[1], ssd_w_in[0], ssd_dt_bias[0])
    _, xbc_c, dt_c = _ssd_in(ctx, mod_c[1], ws, tm)
    z_x, xbc_x, dt_x = _ssd_in(x, mod_x[1], ws, 2 * tm)
    alog = jnp.pad(ssd_a_log[0].reshape(1, -1), ((0, 0), (0, LANES - 2 * SSD_HEADS)))
    dsk = jnp.repeat(ssd_d[0], SSD_HEADDIM).reshape(1, -1)
    y = _ssd(xbc_c, dt_c, xbc_x, dt_x, alog, dsk, ssd_conv_w[0], ssd_conv_b[0].reshape(1, -1))
    wp1 = _post_weights(ssd_w_o[0], norm_ffn[1], ffn, 1)
    return _post_ssd(x, y, z_x, ssd_norm[0].reshape(1, -1), mod_x[1], wp1,
                     final_norm.reshape(1, -1), 2 * tm)
```

```python
import functools
import math

import numpy as np
import jax
import jax.numpy as jnp
from jax import lax
from jax.experimental import pallas as pl
from jax.experimental.pallas import tpu as pltpu

D_MODEL = 1024
DEPTH = 2
GRID_W = 64
EPS = 1e-6
MLA_HEADS = 16
Q_LORA = 512
KV_LORA = 256
QK_NOPE = 64
QK_ROPE = 32
V_HEAD = 64
ROPE_THETA = 10000.0
SSD_INNER = 2 * D_MODEL
SSD_HEADDIM = 64
SSD_HEADS = SSD_INNER // SSD_HEADDIM
SSD_GROUPS = 4
SSD_STATE = 128
SSD_CONV = 5
SSD_CONV_DIM = SSD_INNER + 2 * SSD_GROUPS * SSD_STATE
CHUNK = 128
FFN_HIDDEN = ((8 * D_MODEL // 3 + 255) // 256) * 256

LANES = 128
V7X_VMEM_BYTES = 64 * 1024 * 1024
VMEM_LIMIT_CAP = 56 * 1024 * 1024

F32 = jnp.float32
BF16 = jnp.bfloat16
HEAD_LANES = 2 * QK_NOPE
HEADS_PER_GROUP = SSD_HEADS // SSD_GROUPS
GROUP_W = HEADS_PER_GROUP * SSD_HEADDIM


def _cparams(semantics, vmem_bytes):
    return pltpu.CompilerParams(
        dimension_semantics=semantics,
        vmem_limit_bytes=int(min(VMEM_LIMIT_CAP, vmem_bytes)),
    )


def _const_spec(shape):
    nd = len(shape)
    return pl.BlockSpec(shape, lambda *_: (0,) * nd, pipeline_mode=pl.Buffered(1))


def _rms(x, g):
    return x * lax.rsqrt(jnp.mean(x * x, axis=-1, keepdims=True) + EPS) * g


def _silu(x):
    return x * (1.0 / (1.0 + jnp.exp(-x)))


def _dot(a, b):
    return jnp.dot(a, b, preferred_element_type=F32)


def _dot_nt(a, b):
    return lax.dot_general(a, b, (((1,), (1,)), ((), ())), preferred_element_type=F32)


def _ada_kernel(c_ref, w_ref, b_ref, o_ref):
    s = _silu(c_ref[...])
    s_hi = s.astype(BF16)
    s_lo = (s - s_hi.astype(F32)).astype(BF16)
    w = w_ref[...]
    w_hi = w.astype(BF16)
    w_lo = (w - w_hi.astype(F32)).astype(BF16)
    r = _dot(jnp.concatenate([s_hi, s_lo], axis=0), w_hi)
    o_ref[...] = r[0:8, :] + r[8:16, :] + _dot(s_hi, w_lo) + b_ref[...]


def _ada(cc, ada_w, ada_b):
    width = ada_w.shape[2] // 2
    return pl.pallas_call(
        _ada_kernel,
        grid=(DEPTH, 2),
        in_specs=[
            pl.BlockSpec((8, D_MODEL), lambda l, j: (0, 0)),
            pl.BlockSpec((None, D_MODEL, width), lambda l, j: (l, 0, j)),
            pl.BlockSpec((None, 1, width), lambda l, j: (l, 0, j)),
        ],
        out_specs=pl.BlockSpec((None, 8, width), lambda l, j: (l, 0, j)),
        out_shape=jax.ShapeDtypeStruct((DEPTH, 8, ada_w.shape[2]), F32),
        compiler_params=_cparams(("arbitrary", "arbitrary"), 2 * D_MODEL * width * 4 + (16 << 20)),
        name="ada",
    )(cc, ada_w, ada_b.reshape(DEPTH, 1, -1))


def _mod_spec(mod):
    if mod.shape[0] == 1:
        return pl.BlockSpec((None, 6, D_MODEL), lambda b, i: (0, 0, 0))
    return pl.BlockSpec((None, 6, D_MODEL), lambda b, i: (b, 0, 0))


def _mla_proj_kernel(x_ref, mod_ref, g_ref, win_ref, qg_ref, wq_ref, kvg_ref, wk_ref, wv_ref,
                     tq_ref, tkc_ref, tks_ref, q_ref, k_ref, v_ref, *, sub):
    tiles = [slice(r0, r0 + sub) for r0 in range(0, x_ref.shape[0], sub)]
    hs = [(_rms(x_ref[rows, :], g_ref[...]) * (1.0 + mod_ref[1:2, :]) + mod_ref[0:1, :]).astype(BF16)
          for rows in tiles]
    ps = [_dot(h, win_ref[...]) for h in hs]
    cqns = [_rms(p[:, :Q_LORA], qg_ref[...]).astype(BF16) for p in ps]
    ckvns = [_rms(p[:, Q_LORA:Q_LORA + KV_LORA], kvg_ref[...]).astype(BF16) for p in ps]
    for rows, p, cqn, ckvn in zip(tiles, ps, cqns, ckvns):
        kr_a = p[:, Q_LORA + KV_LORA:Q_LORA + KV_LORA + LANES]
        kr_b = p[:, Q_LORA + KV_LORA + LANES:]
        qf = _dot(cqn, wq_ref[...])
        kf = _dot(ckvn, wk_ref[...])
        kr = kr_a * tkc_ref[rows, :] + kr_b * tks_ref[rows, :]
        tq = tq_ref[rows, :]
        for hd in range(MLA_HEADS):
            sl = slice(hd * HEAD_LANES, (hd + 1) * HEAD_LANES)
            q_ref[hd, rows, :] = (qf[:, sl] * tq).astype(BF16)
            k_ref[hd, rows, :] = (kf[:, sl] + kr).astype(BF16)
        v_ref[rows, :] = _dot(ckvn, wv_ref[...]).astype(BF16)


def _mla_proj(x, mod, gain, w, tabs, tm, sub=256):
    bsz, seq, _ = x.shape
    row = lambda n: pl.BlockSpec((None, tm, n), lambda b, i: (b, i, 0))
    heads = pl.BlockSpec((None, MLA_HEADS, tm, HEAD_LANES), lambda b, i: (b, 0, i, 0))
    tab = pl.BlockSpec((tm, LANES), lambda b, i: (i, 0))
    return pl.pallas_call(
        functools.partial(_mla_proj_kernel, sub=min(sub, tm)),
        grid=(bsz, seq // tm),
        in_specs=[
            row(D_MODEL), _mod_spec(mod), _const_spec((1, D_MODEL)),
            _const_spec(w["win"].shape), _const_spec((1, Q_LORA)), _const_spec(w["wq"].shape),
            _const_spec((1, KV_LORA)), _const_spec(w["wk"].shape), _const_spec(w["wv"].shape),
            tab, tab, tab,
        ],
        out_specs=[heads, heads, row(MLA_HEADS * V_HEAD)],
        out_shape=[
            jax.ShapeDtypeStruct((bsz, MLA_HEADS, seq, HEAD_LANES), BF16),
            jax.ShapeDtypeStruct((bsz, MLA_HEADS, seq, HEAD_LANES), BF16),
            jax.ShapeDtypeStruct((bsz, seq, MLA_HEADS * V_HEAD), BF16),
        ],
        compiler_params=_cparams(("arbitrary", "arbitrary"), 40 << 20),
        name="mla_proj",
    )(x, mod, gain, w["win"], w["qg"], w["wq"], w["kvg"], w["wk"], w["wv"],
      tabs["q"], tabs["kc"], tabs["ks"])


def _attn_kernel(*refs, n_kv, tq, tk):
    q_ref = refs[0]
    kv_refs = [(refs[1 + 2 * j], refs[2 + 2 * j]) for j in range(n_kv)]
    o_ref = refs[1 + 2 * n_kv]
    s_ref, m_ref, vm_ref = refs[2 + 2 * n_kv:]
    n_q = q_ref.shape[1] // tq
    chan = lax.broadcasted_iota(jnp.int32, (2 * V_HEAD, 1), 0)
    low_half = chan < V_HEAD
    chunks = []
    col = 0
    for k_ref, v_ref in kv_refs:
        rows = k_ref.shape[1]
        v_t = v_ref[...].astype(F32).T
        vm_ref[0, :, col:col + rows] = jnp.where(low_half, v_t, 1.0).astype(BF16)
        vm_ref[1, :, col:col + rows] = jnp.where(low_half, 1.0, v_t).astype(BF16)
        step = min(tk, rows)
        for r0 in range(0, rows, step):
            chunks.append((k_ref, r0, step, col + r0))
        col += rows

    def tile_rows(i):
        start = i * tq
        return pl.ds(start if isinstance(start, int) else pl.multiple_of(start, tq), tq)

    def scores(i, hh):
        qh = q_ref[hh, tile_rows(i), :]
        m = None
        for k_ref, r0, rows, c0 in chunks:
            s_t = _dot_nt(k_ref[hh, r0:r0 + rows, :], qh)
            s_ref[hh, c0:c0 + rows, :] = s_t
            cm = jnp.max(s_t, axis=0, keepdims=True)
            m = cm if m is None else jnp.maximum(m, cm)
        m_ref[hh] = jnp.broadcast_to(m, (8, tq))

    def softmax_pv(hh):
        acc = jnp.zeros((2 * V_HEAD, tq), F32)
        for _, _, rows, c0 in chunks:
            p_t = jnp.exp2(s_ref[hh, c0:c0 + rows, :] - m_ref[hh, 0:1, :]).astype(BF16)
            acc = acc + _dot(vm_ref[hh, :, c0:c0 + rows], p_t)
        return acc

    def tile(i, next_scores):
        scores(i, 1)
        acc0 = softmax_pv(0)
        if next_scores:
            scores(i + 1, 0)
        acc1 = softmax_pv(1)
        out_t = jnp.where(low_half, acc0 / acc0[V_HEAD:V_HEAD + 1, :], acc1 / acc1[0:1, :])
        o_ref[tile_rows(i), :] = out_t.T.astype(BF16)

    scores(0, 0)

    def body(i, carry):
        tile(i, True)
        return carry

    lax.fori_loop(0, n_q - 1, body, 0)
    tile(n_q - 1, False)


def _attn(q, kvs, tq, tk):
    bsz, _, lq, _ = q.shape
    n_pairs = MLA_HEADS // 2
    total = sum(k.shape[2] for k, _ in kvs)
    in_specs = [pl.BlockSpec((None, 2, lq, HEAD_LANES), lambda b, h: (b, h, 0, 0))]
    args = [q]
    for k, v in kvs:
        t = k.shape[2]
        in_specs.append(pl.BlockSpec((None, 2, t, HEAD_LANES), lambda b, h: (b, h, 0, 0)))
        in_specs.append(pl.BlockSpec((None, t, 2 * V_HEAD), lambda b, h: (b, 0, h)))
        args += [k, v]
    blocks = 2 * (2 * lq * HEAD_LANES + total * (2 * HEAD_LANES + 2 * V_HEAD) + lq * 2 * V_HEAD) * 2
    vmem = blocks + 2 * tq * (total + LANES) * 4 + 2 * total * 2 * V_HEAD * 2 + (16 << 20)
    return pl.pallas_call(
        functools.partial(_attn_kernel, n_kv=len(kvs), tq=tq, tk=tk),
        grid=(bsz, n_pairs),
        in_specs=in_specs,
        out_specs=pl.BlockSpec((None, lq, 2 * V_HEAD), lambda b, h: (b, 0, h)),
        out_shape=jax.ShapeDtypeStruct((bsz, lq, MLA_HEADS * V_HEAD), BF16),
        scratch_shapes=[pltpu.VMEM((2, total, tq), F32), pltpu.VMEM((2, 8, tq), F32),
                        pltpu.VMEM((2, 2 * V_HEAD, total), BF16)],
        compiler_params=_cparams(("arbitrary", "arbitrary"), vmem),
        name="attn",
    )(*args)


def _resid_ffn(mix, x_ref, mod_ref, wo_ref, gf_ref, w1_ref, w3_ref, w2_ref, fg_ref, o_ref, sub):
    tiles = [slice(r0, r0 + sub) for r0 in range(0, x_ref.shape[0], sub)]
    a = [mix(rows) for rows in tiles]
    x1 = [x_ref[rows, :] + mod_ref[2:3, :] * _dot(ak, wo_ref[...]) for rows, ak in zip(tiles, a)]
    hx = [(_rms(xk, gf_ref[...]) * (1.0 + mod_ref[4:5, :]) + mod_ref[3:4, :]).astype(BF16) for xk in x1]
    t = [(_silu(_dot(hk, w1_ref[...])) * _dot(hk, w3_ref[...])).astype(BF16) for hk in hx]
    for rows, xk, tk_ in zip(tiles, x1, t):
        x2 = xk + mod_ref[5:6, :] * _dot(tk_, w2_ref[...])
        if fg_ref is not None:
            x2 = _rms(x2, fg_ref[...])
        o_ref[rows, :] = x2


def _post_mla_kernel(x_ref, a_ref, mod_ref, wo_ref, gf_ref, w1_ref, w3_ref, w2_ref, o_ref, *, sub):
    _resid_ffn(lambda rows: a_ref[rows, :], x_ref, mod_ref, wo_ref, gf_ref, w1_ref, w3_ref, w2_ref,
               None, o_ref, sub)


def _post_ssd_kernel(x_ref, y_ref, z_ref, ng_ref, mod_ref, wo_ref, gf_ref, w1_ref, w3_ref, w2_ref,
                     fg_ref, o_ref, *, sub):
    def gated_norm(rows):
        gated = y_ref[rows, :].astype(F32) * _silu(z_ref[rows, :].astype(F32))
        return _rms(gated, ng_ref[...]).astype(BF16)

    _resid_ffn(gated_norm, x_ref, mod_ref, wo_ref, gf_ref, w1_ref, w3_ref, w2_ref, fg_ref, o_ref, sub)


def _layer_spec(stacked, layer):
    return pl.BlockSpec((None,) + stacked.shape[1:], lambda *_: (layer, 0, 0),
                        pipeline_mode=pl.Buffered(1))


def _post_weight_specs(w):
    layer = w["layer"]
    return [_const_spec(w["wo"].shape), _const_spec((1, D_MODEL)), _layer_spec(w["w1"], layer),
            _layer_spec(w["w3"], layer), _layer_spec(w["w2"], layer)]


def _post_mla(x, a, mod, w, tm, sub=256):
    bsz, seq, _ = x.shape
    row = lambda n: pl.BlockSpec((None, tm, n), lambda b, i: (b, i, 0))
    return pl.pallas_call(
        functools.partial(_post_mla_kernel, sub=min(sub, tm)),
        grid=(bsz, seq // tm),
        in_specs=[row(D_MODEL), row(a.shape[2]), _mod_spec(mod)] + _post_weight_specs(w),
        out_specs=row(D_MODEL),
        out_shape=jax.ShapeDtypeStruct((bsz, seq, D_MODEL), F32),
        compiler_params=_cparams(("arbitrary", "arbitrary"), VMEM_LIMIT_CAP),
        name="post_mla",
    )(x, a, mod, w["wo"], w["gf"], w["w1"], w["w3"], w["w2"])


def _post_ssd(x, y, z, norm_gain, mod, w, final_gain, tm, sub=256):
    bsz, seq, _ = x.shape
    row = lambda n: pl.BlockSpec((None, tm, n), lambda b, i: (b, i, 0))
    return pl.pallas_call(
        functools.partial(_post_ssd_kernel, sub=min(sub, tm)),
        grid=(bsz, seq // tm),
        in_specs=[row(D_MODEL), row(SSD_INNER), row(SSD_INNER), _const_spec((1, SSD_INNER)),
                  _mod_spec(mod)] + _post_weight_specs(w) + [_const_spec((1, D_MODEL))],
        out_specs=row(D_MODEL),
        out_shape=jax.ShapeDtypeStruct((bsz, seq, D_MODEL), F32),
        compiler_params=_cparams(("arbitrary", "arbitrary"), VMEM_LIMIT_CAP),
        name="post_ssd",
    )(x, y, z, norm_gain, mod, w["wo"], w["gf"], w["w1"], w["w3"], w["w2"], final_gain)


def _ssd_in_kernel(x_ref, mod_ref, g_ref, wz_ref, wx_ref, wdh_ref, wdl_ref, db_ref,
                   z_ref, xbc_ref, dt_ref, *, sub):
    tiles = [slice(r0, r0 + sub) for r0 in range(0, x_ref.shape[0], sub)]
    hs = [_rms(x_ref[rows, :], g_ref[...]) * (1.0 + mod_ref[1:2, :]) + mod_ref[0:1, :] for rows in tiles]
    hbs = [h.astype(BF16) for h in hs]
    for rows, hb in zip(tiles, hbs):
        z_ref[rows, :] = _dot(hb, wz_ref[...]).astype(BF16)
    for rows, hb in zip(tiles, hbs):
        xbc_ref[rows, :] = _dot(hb, wx_ref[...]).astype(BF16)
    for rows, h, hb in zip(tiles, hs, hbs):
        hl = (h - hb.astype(F32)).astype(BF16)
        raw = _dot(hb, wdh_ref[...]) + _dot(hl, wdh_ref[...]) + _dot(hb, wdl_ref[...])
        raw = raw + db_ref[...]
        dt_ref[rows, :] = jnp.maximum(raw, 0.0) + jnp.log(1.0 + jnp.exp(-jnp.abs(raw)))


def _ssd_in(x, mod, w, tm, sub=256):
    bsz, seq, _ = x.shape
    row = lambda n: pl.BlockSpec((None, tm, n), lambda b, i: (b, i, 0))
    return pl.pallas_call(
        functools.partial(_ssd_in_kernel, sub=min(sub, tm)),
        grid=(bsz, seq // tm),
        in_specs=[row(D_MODEL), _mod_spec(mod), _const_spec((1, D_MODEL)),
                  _const_spec(w["wz"].shape), _const_spec(w["wx"].shape),
                  _const_spec(w["wdh"].shape), _const_spec(w["wdl"].shape),
                  _const_spec((1, LANES))],
        out_specs=[row(SSD_INNER), row(SSD_CONV_DIM), row(LANES)],
        out_shape=[jax.ShapeDtypeStruct((bsz, seq, SSD_INNER), BF16),
                   jax.ShapeDtypeStruct((bsz, seq, SSD_CONV_DIM), BF16),
                   jax.ShapeDtypeStruct((bsz, seq, LANES), F32)],
        compiler_params=_cparams(("arbitrary", "arbitrary"), 48 << 20),
        name="ssd_in",
    )(x, mod, w["g"], w["wz"], w["wx"], w["wdh"], w["wdl"], w["db"])


CONV_PAD = 16
LOG2E = math.log2(math.e)


def _ssd_kernel(xc_ref, bc_ref, cc_ref, dtc_ref, xx_ref, bx_ref, cx_ref, dtx_ref,
                alog_ref, dsk_ref, cwx_ref, cwb_ref, cwc_ref, cbx_ref, cbb_ref, cbc_ref,
                y_ref, s_ref, cum_ref, cumt_ref, dtt_ref, btr_ref, ux_ref, ub_ref, uc_ref, shift_ref):
    g = pl.program_id(1)
    n_ctx = xc_ref.shape[0] // CHUNK
    n_lat = xx_ref.shape[0] // CHUNK
    a_all = jnp.broadcast_to(-jnp.exp(alog_ref[...]), (8, LANES))
    qi = lax.broadcasted_iota(jnp.int32, (CHUNK, CHUNK), 0)
    ki = lax.broadcasted_iota(jnp.int32, (CHUNK, CHUNK), 1)
    lane = lax.broadcasted_iota(jnp.int32, (1, LANES), 1)
    low_half = lane < SSD_HEADDIM
    n_pair = HEADS_PER_GROUP // 2

    def pair_cols(v, pr):
        return jnp.where(low_half, v[:, 2 * pr:2 * pr + 1], v[:, 2 * pr + 1:2 * pr + 2])

    def split3(v):
        hi = v.astype(BF16)
        r1 = v - hi.astype(F32)
        mid = r1.astype(BF16)
        return hi, mid, (r1 - mid.astype(F32)).astype(BF16)

    def lane_bcast(v, col):
        return jnp.broadcast_to(v[:, col:col + 1], (v.shape[0], LANES))

    sel_r = lax.broadcasted_iota(jnp.int32, (CHUNK, CHUNK + 2 * CONV_PAD), 0)
    sel_c = lax.broadcasted_iota(jnp.int32, (CHUNK, CHUNK + 2 * CONV_PAD), 1)
    shifted_taps = [j for j in range(SSD_CONV) if j != SSD_CONV // 2]
    for n, j in enumerate(shifted_taps):
        hit = sel_c == sel_r + (CONV_PAD - SSD_CONV // 2 + j)
        shift_ref[n * CHUNK:(n + 1) * CHUNK, :] = jnp.where(hit, 1.0, 0.0).astype(BF16)

    def window(raw_ref, r0, first, last):
        pad = jnp.zeros((CONV_PAD, raw_ref.shape[1]), BF16)
        if first and last:
            return jnp.concatenate([pad, raw_ref[0:CHUNK, :], pad], axis=0)
        if first:
            return jnp.concatenate([pad, raw_ref[0:CHUNK + CONV_PAD, :]], axis=0)
        if last:
            return jnp.concatenate([raw_ref[r0 - CONV_PAD:r0 + CHUNK, :], pad], axis=0)
        return raw_ref[pl.ds(pl.multiple_of(r0 - CONV_PAD, CONV_PAD), CHUNK + 2 * CONV_PAD), :]

    def scan_sums(x_ref, b_ref, c_ref, dt_ref, r0, chunk_id, first, last):
        grow = pl.ds(chunk_id * CHUNK if isinstance(chunk_id, int)
                     else pl.multiple_of(chunk_id * CHUNK, CHUNK), CHUNK)
        win = jnp.concatenate([window(r, r0, first, last) for r in (x_ref, b_ref, c_ref)], axis=1)
        taps = _dot(shift_ref[...], win)
        cw = jnp.concatenate([cwx_ref[...], cwb_ref[...], cwc_ref[...]], axis=1)
        acc = jnp.concatenate([cbx_ref[...], cbb_ref[...], cbc_ref[...]], axis=1)
        mid = SSD_CONV // 2
        acc = acc + win[CONV_PAD:CONV_PAD + CHUNK, :].astype(F32) * cw[mid:mid + 1, :]
        for n, j in enumerate(shifted_taps):
            acc = acc + taps[n * CHUNK:(n + 1) * CHUNK, :] * cw[j:j + 1, :]
        u = _silu(acc).astype(BF16)
        ub = u[:, GROUP_W:GROUP_W + SSD_STATE]
        ux_ref[grow, :] = u[:, :GROUP_W]
        ub_ref[grow, :] = ub
        uc_ref[grow, :] = u[:, GROUP_W + SSD_STATE:]
        btr_ref[chunk_id] = ub.astype(F32).T
        dt = dt_ref[pl.ds(r0, CHUNK), :]
        for direction in range(2):
            causal = (ki <= qi) if direction == 0 else (ki >= qi)
            shift = (LANES - (direction * SSD_HEADS + g * HEADS_PER_GROUP)) & (LANES - 1)
            dts = pltpu.roll(dt, shift, 1)
            a = dts * pltpu.roll(a_all, shift, 1)[0:1, :]
            tri = jnp.where(causal, 1.0, 0.0).astype(BF16)
            r = _dot(tri, jnp.concatenate(split3(a), axis=1))
            c = (r[:, :LANES] + r[:, LANES:2 * LANES] + r[:, 2 * LANES:]) * LOG2E
            cum_ref[direction, chunk_id] = c
            cumt_ref[direction, chunk_id] = c.T[0:8, :]
            dtt_ref[direction, chunk_id] = dts.T[0:8, :]

    def scan_step(chunk_id, direction, emit):
        grow = pl.ds(chunk_id * CHUNK if isinstance(chunk_id, int)
                     else pl.multiple_of(chunk_id * CHUNK, CHUNK), CHUNK)
        st_ref = s_ref.at[direction]
        causal = (ki <= qi) if direction == 0 else (ki >= qi)
        c = cum_ref[direction, chunk_id]
        c_t = cumt_ref[direction, chunk_id]
        dt_t = dtt_ref[direction, chunk_id]
        end = CHUNK - 1 if direction == 0 else 0
        to_end_t = jnp.exp2(c_t[:, end:end + 1] - c_t) * dt_t
        cd = jnp.exp2(c[end:end + 1, :])
        b_t = btr_ref[chunk_id]
        x = ux_ref[grow, :]
        if emit:
            cmat = uc_ref[grow, :]
            cb = _dot_nt(cmat, ub_ref[grow, :])
            y_off = _dot(cmat, st_ref[...].astype(BF16))
        ys, upds = [], []
        for pr in range(n_pair):
            heads = (2 * pr, 2 * pr + 1)
            ps = slice(pr * LANES, (pr + 1) * LANES)
            xp = x[:, ps]
            zero = jnp.zeros_like(xp)
            rhs = jnp.concatenate([jnp.where(low_half, xp, zero),
                                   jnp.where(low_half, zero, xp)], axis=0)
            bts = jnp.concatenate([(b_t * to_end_t[hd:hd + 1, :]).astype(BF16) for hd in heads], axis=1)
            upds.append(_dot(bts, rhs))
            if emit:
                ws, e_b = [], []
                for hd in heads:
                    cq = lane_bcast(c, hd)
                    dec = jnp.exp2(jnp.where(causal, cq - c_t[hd:hd + 1, :], -jnp.inf))
                    ws.append((cb * dec * dt_t[hd:hd + 1, :]).astype(BF16))
                    e_b.append(jnp.exp2(cq))
                ys.append(_dot(jnp.concatenate(ws, axis=1), rhs)
                          + jnp.where(low_half, e_b[0], e_b[1]) * y_off[:, ps])
        cds = jnp.concatenate([pair_cols(cd, pr) for pr in range(n_pair)], axis=1)
        st_ref[...] = st_ref[...] * cds + jnp.concatenate(upds, axis=1)
        return (jnp.concatenate(ys, axis=1), x) if emit else (None, None)

    dsk = dsk_ref[...]
    s_ref[...] = jnp.zeros_like(s_ref)
    for ci in range(n_ctx):
        scan_sums(xc_ref, bc_ref, cc_ref, dtc_ref, ci * CHUNK, ci, ci == 0, ci == n_ctx - 1)

    assert n_lat >= 2
    scan_sums(xx_ref, bx_ref, cx_ref, dtx_ref, 0, n_ctx, True, False)

    def sums_body(j, carry):
        scan_sums(xx_ref, bx_ref, cx_ref, dtx_ref, pl.multiple_of(j * CHUNK, CHUNK), n_ctx + j,
                  False, False)
        return carry

    lax.fori_loop(1, n_lat - 1, sums_body, 0, unroll=3)
    scan_sums(xx_ref, bx_ref, cx_ref, dtx_ref, (n_lat - 1) * CHUNK, n_ctx + n_lat - 1, False, True)

    for ci in range(n_ctx):
        for direction, cj in ((0, ci), (1, n_ctx - 1 - ci)):
            scan_step(cj, direction, False)

    def body(i, carry, first_touch):
        for direction in range(2):
            chunk = i if direction == 0 else n_lat - 1 - i
            rows = pl.ds(pl.multiple_of(chunk * CHUNK, CHUNK), CHUNK)
            y, x = scan_step(n_ctx + chunk, direction, True)
            if direction == 0:
                y = y + dsk * x.astype(F32)
            if not first_touch:
                y = y + y_ref[rows, :].astype(F32)
            y_ref[rows, :] = y.astype(BF16)
        return carry

    assert n_lat % 2 == 0
    lax.fori_loop(0, n_lat // 2, functools.partial(body, first_touch=True), 0, unroll=4)
    lax.fori_loop(n_lat // 2, n_lat, functools.partial(body, first_touch=False), 0, unroll=4)


def _ssd(uc, dtc, ux, dtx, alog, dsk, conv_w, conv_b):
    bsz, lc, _ = uc.shape
    lx = ux.shape[1]
    n_chunks = (lc + lx) // CHUNK
    b_blk0 = SSD_INNER // SSD_STATE
    c_blk0 = b_blk0 + SSD_GROUPS
    chan_specs = [(GROUP_W, lambda b, g: (0, g)),
                  (SSD_STATE, lambda b, g: (0, b_blk0 + g)),
                  (SSD_STATE, lambda b, g: (0, c_blk0 + g))]

    def seg_specs(rows):
        return [pl.BlockSpec((None, rows, GROUP_W), lambda b, g: (b, 0, g)),
                pl.BlockSpec((None, rows, SSD_STATE), lambda b, g: (b, 0, b_blk0 + g)),
                pl.BlockSpec((None, rows, SSD_STATE), lambda b, g: (b, 0, c_blk0 + g)),
                pl.BlockSpec((None, rows, LANES), lambda b, g: (b, 0, 0))]

    return pl.pallas_call(
        _ssd_kernel,
        grid=(bsz, SSD_GROUPS),
        in_specs=seg_specs(lc) + seg_specs(lx) + [
            pl.BlockSpec((1, LANES), lambda b, g: (0, 0)),
            pl.BlockSpec((1, GROUP_W), lambda b, g: (0, g))]
        + [pl.BlockSpec((SSD_CONV, wd), f) for wd, f in chan_specs]
        + [pl.BlockSpec((1, wd), f) for wd, f in chan_specs],
        out_specs=pl.BlockSpec((None, lx, GROUP_W), lambda b, g: (b, 0, g)),
        out_shape=jax.ShapeDtypeStruct((bsz, lx, SSD_INNER), BF16),
        scratch_shapes=[
            pltpu.VMEM((2, SSD_STATE, GROUP_W), F32),
            pltpu.VMEM((2, n_chunks, CHUNK, LANES), F32),
            pltpu.VMEM((2, n_chunks, 8, CHUNK), F32),
            pltpu.VMEM((2, n_chunks, 8, CHUNK), F32),
            pltpu.VMEM((n_chunks, SSD_STATE, CHUNK), F32),
            pltpu.VMEM((lc + lx, GROUP_W), BF16),
            pltpu.VMEM((lc + lx, SSD_STATE), BF16),
            pltpu.VMEM((lc + lx, SSD_STATE), BF16),
            pltpu.VMEM(((SSD_CONV - 1) * CHUNK, CHUNK + 2 * CONV_PAD), BF16),
        ],
        compiler_params=_cparams(("arbitrary", "arbitrary"), 54 << 20),
        name="ssd",
    )(uc, uc, uc, dtc, ux, ux, ux, dtx, alog, dsk, conv_w, conv_w, conv_w, conv_b, conv_b, conv_b)


def _rope_partner_index():
    j = np.arange(QK_ROPE)
    return np.where(j % 16 < 8, j + 8, j - 8)


def _rope_tables(seq):
    t = np.arange(seq)
    pos = np.stack([t // GRID_W, t % GRID_W], axis=1).astype(np.float64)
    j = np.arange(QK_ROPE)
    inv = ROPE_THETA ** (-np.arange(8, dtype=np.float64) * 2.0 / 16.0)
    ang = pos[:, j // 16] * inv[j % 8][None, :]
    sign = np.where(j % 16 < 8, -1.0, 1.0)
    return np.cos(ang), np.sin(ang) * sign


def _mla_tables(seq, rotate):
    scale = math.log2(math.e) / math.sqrt(QK_NOPE + QK_ROPE)
    if rotate:
        cos, sin = _rope_tables(seq)
    else:
        cos, sin = np.ones((seq, QK_ROPE)), np.zeros((seq, QK_ROPE))
    one = np.ones((seq, QK_NOPE))
    zero = np.zeros((seq, QK_NOPE))
    tabs = {"q": np.concatenate([one, cos, sin], axis=1) * scale,
            "kc": np.concatenate([zero, cos, cos], axis=1),
            "ks": np.concatenate([zero, sin, sin], axis=1)}
    return {k: jnp.asarray(v.astype(np.float32)) for k, v in tabs.items()}


def _mla_weights(w_in, q_norm, w_uq, kv_norm, w_ukv):
    partner = _rope_partner_index()
    kr = w_in[:, Q_LORA + KV_LORA:]
    krp = kr[:, partner]
    zero = jnp.zeros((D_MODEL, QK_NOPE), F32)
    win = jnp.concatenate([w_in[:, :Q_LORA + KV_LORA], zero, kr, kr, zero, krp, krp], axis=1)
    uq = w_uq.reshape(Q_LORA, MLA_HEADS, QK_NOPE + QK_ROPE)
    rope = uq[:, :, QK_NOPE:]
    wq = jnp.concatenate([uq, rope[:, :, partner]], axis=2).reshape(Q_LORA, MLA_HEADS * HEAD_LANES)
    ukv = w_ukv.reshape(KV_LORA, MLA_HEADS, QK_NOPE + V_HEAD)
    wk = jnp.concatenate([ukv[:, :, :QK_NOPE], jnp.zeros((KV_LORA, MLA_HEADS, QK_NOPE), F32)], axis=2)
    return {"win": win.astype(BF16), "qg": q_norm.reshape(1, -1), "wq": wq.astype(BF16),
            "kvg": kv_norm.reshape(1, -1), "wk": wk.reshape(KV_LORA, -1).astype(BF16),
            "wv": ukv[:, :, QK_NOPE:].reshape(KV_LORA, -1).astype(BF16)}


def _post_weights(wo, gf, ffn, layer):
    return {"wo": wo.astype(BF16), "gf": gf.reshape(1, -1), "layer": layer, **ffn}


def _ssd_in_weights(gain, w_in, dt_bias):
    wd = jnp.pad(w_in[:, SSD_INNER + SSD_CONV_DIM:], ((0, 0), (0, LANES - 2 * SSD_HEADS)))
    wdh = wd.astype(BF16)
    wdl = (wd - wdh.astype(F32)).astype(BF16)
    db = jnp.pad(dt_bias.reshape(1, -1), ((0, 0), (0, LANES - 2 * SSD_HEADS)))
    return {"g": gain.reshape(1, -1), "wz": w_in[:, :SSD_INNER].astype(BF16),
            "wx": w_in[:, SSD_INNER:SSD_INNER + SSD_CONV_DIM].astype(BF16),
            "wdh": wdh, "wdl": wdl, "db": db}


def kernel(x, c, ctx, c_ctx, ada_w, ada_b, norm_mix, norm_ffn, ffn_w1, ffn_w3, ffn_w2,
           mla_w_in, mla_q_norm, mla_w_uq, mla_kv_norm, mla_w_ukv, mla_w_o,
           ssd_w_in, ssd_conv_w, ssd_conv_b, ssd_dt_bias, ssd_a_log, ssd_d, ssd_norm, ssd_w_o,
           final_norm):
    bsz, seq, _ = x.shape
    lc = ctx.shape[1]
    tm = 256

    cc = jnp.zeros((8, D_MODEL), F32).at[:bsz].set(c).at[bsz].set(c_ctx)
    mod = _ada(cc, ada_w, ada_b)
    mod_x = [mod[l, :bsz].reshape(bsz, 6, D_MODEL) for l in range(DEPTH)]
    mod_c = [mod[l, bsz:bsz + 1].reshape(1, 6, D_MODEL) for l in range(DEPTH)]

    wm = _mla_weights(mla_w_in[0], mla_q_norm[0], mla_w_uq[0], mla_kv_norm[0], mla_w_ukv[0])
    g0 = norm_mix[0].reshape(1, -1)
    q_c, k_c, v_c = _mla_proj(ctx, mod_c[0], g0, wm, _mla_tables(lc, False), tm)
    q_x, k_x, v_x = _mla_proj(x, mod_x[0], g0, wm, _mla_tables(seq, True), 2 * tm)
    o_x = _attn(q_x, [(k_c, v_c), (k_x, v_x)], tq=256, tk=512)
    o_c = _attn(q_c, [(k_c, v_c)], tq=lc, tk=512)
    ffn = {"w1": ffn_w1.astype(BF16), "w3": ffn_w3.astype(BF16), "w2": ffn_w2.astype(BF16)}
    wp0 = _post_weights(mla_w_o[0], norm_ffn[0], ffn, 0)
    x = _post_mla(x, o_x, mod_x[0], wp0, 2 * tm)
    ctx = _post_mla(ctx, o_c, mod_c[0], wp0, tm)

    ws = _ssd_in_weights(norm_mix[1], ssd_w_in[0], ssd_dt_bias[0])
    _, xbc_c, dt_c = _ssd_in(ctx, mod_c[1], ws, tm)
    z_x, xbc_x, dt_x = _ssd_in(x, mod_x[1], ws, 2 * tm)
    alog = jnp.pad(ssd_a_log[0].reshape(1, -1), ((0, 0), (0, LANES - 2 * SSD_HEADS)))
    dsk = jnp.repeat(ssd_d[0], SSD_HEADDIM).reshape(1, -1)
    y = _ssd(xbc_c, dt_c, xbc_x, dt_x, alog, dsk, ssd_conv_w[0], ssd_conv_b[0].reshape(1, -1))
    wp1 = _post_weights(ssd_w_o[0], norm_ffn[1], ffn, 1)
    return _post_ssd(x, y, z_x, ssd_norm[0].reshape(1, -1), mod_x[1], wp1,
                     final_norm.reshape(1, -1), 2 * tm)
```
